```python
import math
import jax
import jax.numpy as jnp
from jax import lax
import numpy as np


D_MODEL = 4096
BATCH = 4
SEQ = 2048
DEPTH = 2
DEC_BATCH = 8
DEC_SEQ = 4
PAST_LEN = 16384
PAGE_SIZE = 128

HEAD_DIM = 128
MIX_WIDTH = D_MODEL
CONV_CH = MIX_WIDTH // 4
ATTN_HEADS = (3 * MIX_WIDTH // 8) // HEAD_DIM
MLSTM_HEADS = (MIX_WIDTH - CONV_CH - ATTN_HEADS * HEAD_DIM) // HEAD_DIM
ATTN_W = ATTN_HEADS * HEAD_DIM
MLSTM_W = MLSTM_HEADS * HEAD_DIM
CONV_WIDTH = 31
DILATED_PATTERNS = ((128, 1), (512, 4), (2048, 16))
MAX_WINDOW = 2048
ATTN_Q_BLOCK = 128
MLSTM_CHUNK = 64
PEER_HEADS = 8
PEER_N_KEYS = 128
PEER_N_EXPERTS = PEER_N_KEYS * PEER_N_KEYS
PEER_QUERY_DIM = 256
PEER_HALF = PEER_QUERY_DIM // 2
PEER_TOPK = 16
PEER_TOKEN_BLOCK = 128
DEEPNORM_ALPHA = (2 * DEPTH) ** 0.25
DEEPNORM_BETA = (8 * DEPTH) ** -0.25
LN_EPS = 1e-5

OFF_GLU = 0
OFF_ATT = OFF_GLU + 2 * CONV_CH
OFF_MLS = OFF_ATT + 3 * ATTN_W
OFF_GATE = OFF_MLS + 4 * MLSTM_W
P_IN = OFF_GATE + 2 * MLSTM_HEADS

kernel_name = "hymba_conv_dilattn_mlstm_peer_step"


def layer_norm(x, g, b):
    xf = x.astype(jnp.float32)
    mu = jnp.mean(xf, -1, keepdims=True)
    var = jnp.mean(jnp.square(xf - mu), -1, keepdims=True)
    return ((xf - mu) * lax.rsqrt(var + LN_EPS) * g + b).astype(x.dtype)


def head_norm(h, g):
    hf = h.astype(jnp.float32)
    mu = jnp.mean(hf, -1, keepdims=True)
    var = jnp.mean(jnp.square(hf - mu), -1, keepdims=True)
    return ((hf - mu) * lax.rsqrt(var + LN_EPS) * g).astype(h.dtype)


def conformer_conv(u, prev, w, b, g, beta):
    ext = jnp.concatenate([prev.astype(u.dtype), u], axis=1)
    y = lax.conv_general_dilated(ext, w[:, None, :].astype(u.dtype), window_strides=(1,), padding='VALID',
                                 dimension_numbers=('NWC', 'WIO', 'NWC'),
                                 feature_group_count=u.shape[-1]) + b
    y = jax.nn.silu(layer_norm(y, g, beta))
    return y, ext[:, -(CONV_WIDTH - 1):]


def dilated_attention(q, k_ext, v_ext, n_prev):
    B, T, H, hd = q.shape
    qb = ATTN_Q_BLOCK if T % ATTN_Q_BLOCK == 0 else T
    nb = T // qb
    q_blocks = q.reshape(B, nb, qb, H, hd).transpose(1, 0, 2, 3, 4)
    q_idx = (n_prev + jnp.arange(T, dtype=jnp.int32)).reshape(nb, qb)
    scale = hd ** -0.5

    def one_block(args):
        qblk, qi = args
        outs, lses = [], []
        for window, dil in DILATED_PATTERNS:
            offs = dil * jnp.arange(window // dil + 1, dtype=jnp.int32)
            kidx = qi[:, None] - offs[None, :]
            valid = kidx >= 0
            kidx = jnp.maximum(kidx, 0)
            kg = k_ext[:, kidx]
            vg = v_ext[:, kidx]
            s = jnp.einsum('bqhd,bqjhd->bhqj', qblk, kg).astype(jnp.float32) * scale
            s = jnp.where(valid[None, None], s, -jnp.inf)
            mx = jnp.max(s, -1, keepdims=True)
            e = jnp.exp(s - mx)
            den = jnp.sum(e, -1, keepdims=True)
            p = (e / den).astype(vg.dtype)
            outs.append(jnp.einsum('bhqj,bqjhd->bqhd', p, vg).astype(jnp.float32))
            lses.append((mx + jnp.log(den))[..., 0])
        wts = jax.nn.softmax(jnp.stack(lses), axis=0).transpose(0, 1, 3, 2)[..., None]
        return jnp.sum(wts * jnp.stack(outs), axis=0).astype(q.dtype)

    out = lax.map(one_block, (q_blocks, q_idx))
    return out.transpose(1, 0, 2, 3, 4).reshape(B, T, H, hd)


def mlstm_chunkwise(q, k, v, i_pre, logf, c0, n0, m0):
    B, T, H, hd = q.shape
    L = MLSTM_CHUNK if T % MLSTM_CHUNK == 0 else T
    nc = T // L
    f32 = jnp.float32

    def to_chunks(a):
        a = a.reshape((B, nc, L) + a.shape[2:])
        return jnp.moveaxis(jnp.moveaxis(a, 1, 0), 3, 2)

    qc = to_chunks(q.astype(f32))
    kc = to_chunks(k.astype(f32) * hd ** -0.5)
    vc = to_chunks(v.astype(f32))
    ic = to_chunks(i_pre.astype(f32))
    fc = to_chunks(logf.astype(f32))
    causal = jnp.tril(jnp.ones((L, L), dtype=bool))

    def step(carry, inp):
        c, n, m = carry
        qt, kt, vt, it, ft = inp
        b = jnp.cumsum(ft, axis=-1)
        a_inter = b + m[..., None]
        a_intra = jnp.where(causal, b[..., :, None] - b[..., None, :] + it[..., None, :], -jnp.inf)
        m_t = jnp.maximum(a_inter, jnp.max(a_intra, -1))
        w_intra = jnp.exp(a_intra - m_t[..., None])
        w_inter = jnp.exp(a_inter - m_t)
        qk = jnp.einsum('bhtd,bhsd->bhts', qt, kt) * w_intra
        num = w_inter[..., None] * jnp.einsum('bhtd,bhde->bhte', qt, c) + jnp.einsum('bhts,bhse->bhte', qk, vt)
        qn = w_inter * jnp.einsum('bhtd,bhd->bht', qt, n) + jnp.sum(qk, -1)
        h = num / jnp.maximum(jnp.abs(qn), jnp.exp(-m_t))[..., None]
        m_new = m_t[..., -1]
        decay = jnp.exp(b[..., -1] + m - m_new)
        g = jnp.exp(b[..., -1:] - b + it - m_new[..., None])
        c_new = decay[..., None, None] * c + jnp.einsum('bhs,bhsd,bhse->bhde', g, kt, vt)
        n_new = decay[..., None] * n + jnp.einsum('bhs,bhsd->bhd', g, kt)
        return (c_new, n_new, m_new), h

    (c, n, m), h = lax.scan(step, (c0.astype(f32), n0.astype(f32), m0.astype(f32)), (qc, kc, vc, ic, fc))
    h = jnp.moveaxis(jnp.moveaxis(h, 2, 3), 0, 1).reshape(B, T, H, hd)
    return h.astype(q.dtype), c.astype(c0.dtype), n.astype(n0.dtype), m.astype(m0.dtype)


def peer(x, wq, subkeys, u_tab, v_tab):
    B, T, D = x.shape
    xt = x.reshape(B * T, D)
    n_tok = B * T
    nb = -(-n_tok // PEER_TOKEN_BLOCK)
    xt = jnp.pad(xt, ((0, nb * PEER_TOKEN_BLOCK - n_tok), (0, 0))).reshape(nb, PEER_TOKEN_BLOCK, D)

    def one_block(xb):
        q = (xb @ wq).reshape(PEER_TOKEN_BLOCK, PEER_HEADS, 2, PEER_HALF)
        s = jnp.einsum('nhpd,hpkd->nhpk', q, subkeys).astype(jnp.float32)
        sv, si = lax.top_k(s, PEER_TOPK)
        cand = sv[:, :, 0, :, None] + sv[:, :, 1, None, :]
        cid = si[:, :, 0, :, None] * PEER_N_KEYS + si[:, :, 1, None, :]
        cv, ci = lax.top_k(cand.reshape(PEER_TOKEN_BLOCK, PEER_HEADS, -1), PEER_TOPK)
        eid = jnp.take_along_axis(cid.reshape(PEER_TOKEN_BLOCK, PEER_HEADS, -1), ci, axis=-1)
        gate = jax.nn.softmax(cv, axis=-1)
        ug = u_tab[eid]
        vg = v_tab[eid]
        act = jax.nn.gelu(jnp.einsum('nd,nhkd->nhk', xb, ug).astype(jnp.float32), approximate=False)
        return jnp.einsum('nhk,nhkd->nd', (gate * act).astype(vg.dtype), vg)

    out = lax.map(one_block, xt).reshape(nb * PEER_TOKEN_BLOCK, D)[:n_tok]
    return out.reshape(B, T, D).astype(x.dtype)


def trunk_layer(x, k_prev, v_prev, conv_prev, c0, n0, m0,
                w_in, conv_w, conv_b, conv_ln_g, conv_ln_b, b_i, b_f, norm_g, w_out,
                ln1_g, ln1_b, peer_wq, peer_subkeys, peer_u, peer_v, ln2_g, ln2_b):
    B, T, _ = x.shape
    proj = jnp.einsum('btd,dp->btp', x, w_in)
    glu = proj[..., OFF_GLU:OFF_GLU + CONV_CH] * jax.nn.sigmoid(proj[..., OFF_GLU + CONV_CH:OFF_ATT])
    ya, conv_new = conformer_conv(glu, conv_prev, conv_w, conv_b, conv_ln_g, conv_ln_b)
    qkv = proj[..., OFF_ATT:OFF_MLS].reshape(B, T, 3, ATTN_HEADS, HEAD_DIM)
    q, k, v = qkv[:, :, 0], qkv[:, :, 1], qkv[:, :, 2]
    k_ext = jnp.concatenate([k_prev.astype(k.dtype), k], axis=1)
    v_ext = jnp.concatenate([v_prev.astype(v.dtype), v], axis=1)
    yb = dilated_attention(q, k_ext, v_ext, k_prev.shape[1]).reshape(B, T, ATTN_W)
    mq = proj[..., OFF_MLS:OFF_GATE].reshape(B, T, 4, MLSTM_HEADS, HEAD_DIM)
    gates = proj[..., OFF_GATE:].astype(jnp.float32).reshape(B, T, 2, MLSTM_HEADS)
    i_pre = gates[:, :, 0] + b_i
    logf = jax.nn.log_sigmoid(gates[:, :, 1] + b_f)
    h, c_new, n_new, m_new = mlstm_chunkwise(mq[:, :, 0], mq[:, :, 1], mq[:, :, 2], i_pre, logf, c0, n0, m0)
    yc = head_norm(jax.nn.sigmoid(mq[:, :, 3]) * h, norm_g).reshape(B, T, MLSTM_W)
    mix = jnp.einsum('btm,md->btd', jnp.concatenate([ya, yb, yc], axis=-1), w_out)
    x = layer_norm(DEEPNORM_ALPHA * x + mix, ln1_g, ln1_b)
    x = layer_norm(DEEPNORM_ALPHA * x + peer(x, peer_wq, peer_subkeys, peer_u, peer_v), ln2_g, ln2_b)
    return x, k, v, conv_new, c_new, n_new, m_new


def setup_inputs(seed: int = 0) -> dict:
    key = jax.random.key(seed)
    ks = jax.random.split(key, 32)
    f32 = jnp.float32

    def nrm(k, shape, s):
        return jax.random.normal(k, shape, f32) * s

    win_rows = min(MAX_WINDOW, PAST_LEN)
    return {
        'x_prompt': nrm(ks[0], (BATCH, SEQ, D_MODEL), 1.0),
        'x_sample': nrm(ks[1], (DEC_BATCH, DEC_SEQ, D_MODEL), 1.0),
        'cache_attn_k': nrm(ks[2], (DEPTH, DEC_BATCH, win_rows, ATTN_HEADS, HEAD_DIM), 1.0),
        'cache_attn_v': nrm(ks[3], (DEPTH, DEC_BATCH, win_rows, ATTN_HEADS, HEAD_DIM), 1.0),
        'state_conv': nrm(ks[4], (DEPTH, DEC_BATCH, CONV_WIDTH - 1, CONV_CH), 0.5),
        'state_mlstm_c': nrm(ks[5], (DEPTH, DEC_BATCH, MLSTM_HEADS, HEAD_DIM, HEAD_DIM), 0.1),
        'state_mlstm_n': nrm(ks[6], (DEPTH, DEC_BATCH, MLSTM_HEADS, HEAD_DIM), 0.1),
        'state_mlstm_m': nrm(ks[7], (DEPTH, DEC_BATCH, MLSTM_HEADS), 0.5),
        'w_in': nrm(ks[8], (DEPTH, D_MODEL, P_IN), D_MODEL ** -0.5),
        'conv_w': nrm(ks[9], (DEPTH, CONV_WIDTH, CONV_CH), CONV_WIDTH ** -0.5),
        'conv_b': nrm(ks[10], (DEPTH, CONV_CH), 0.01),
        'conv_ln_g': 1.0 + nrm(ks[11], (DEPTH, CONV_CH), 0.02),
        'conv_ln_b': nrm(ks[12], (DEPTH, CONV_CH), 0.01),
        'mlstm_b_i': nrm(ks[13], (DEPTH, MLSTM_HEADS), 0.1),
        'mlstm_b_f': 3.0 + nrm(ks[14], (DEPTH, MLSTM_HEADS), 0.5),
        'mlstm_norm_g': 1.0 + nrm(ks[15], (DEPTH, MLSTM_HEADS, HEAD_DIM), 0.02),
        'w_out': nrm(ks[16], (DEPTH, MIX_WIDTH, D_MODEL), DEEPNORM_BETA * MIX_WIDTH ** -0.5),
        'ln1_g': 1.0 + nrm(ks[17], (DEPTH, D_MODEL), 0.02),
        'ln1_b': nrm(ks[18], (DEPTH, D_MODEL), 0.01),
        'peer_wq': nrm(ks[19], (DEPTH, D_MODEL, PEER_HEADS * PEER_QUERY_DIM), D_MODEL ** -0.5),
        'peer_subkeys': nrm(ks[20], (DEPTH, PEER_HEADS, 2, PEER_N_KEYS, PEER_HALF), PEER_HALF ** -0.5),
        'peer_u': nrm(ks[21], (DEPTH, PEER_N_EXPERTS, D_MODEL), D_MODEL ** -0.5),
        'peer_v': nrm(ks[22], (DEPTH, PEER_N_EXPERTS, D_MODEL), DEEPNORM_BETA * (PEER_HEADS * PEER_TOPK) ** -0.5),
        'ln2_g': 1.0 + nrm(ks[23], (DEPTH, D_MODEL), 0.02),
        'ln2_b': nrm(ks[24], (DEPTH, D_MODEL), 0.01),
    }


def reference(x_prompt, x_sample, cache_attn_k, cache_attn_v, state_conv, state_mlstm_c, state_mlstm_n,
              state_mlstm_m, w_in, conv_w, conv_b, conv_ln_g, conv_ln_b, mlstm_b_i, mlstm_b_f, mlstm_norm_g,
              w_out, ln1_g, ln1_b, peer_wq, peer_subkeys, peer_u, peer_v, ln2_g, ln2_b):
    B, T, _ = x_prompt.shape
    dt = x_prompt.dtype
    empty_kv = jnp.zeros((B, 0, ATTN_HEADS, HEAD_DIM), dt)
    conv_zero = jnp.zeros((B, CONV_WIDTH - 1, CONV_CH), dt)
    c_zero = jnp.zeros((B, MLSTM_HEADS, HEAD_DIM, HEAD_DIM), jnp.float32)
    n_zero = jnp.zeros((B, MLSTM_HEADS, HEAD_DIM), jnp.float32)
    m_zero = jnp.zeros((B, MLSTM_HEADS), jnp.float32)
    win_p = min(MAX_WINDOW, T)

    yp, ys = x_prompt, x_sample
    kp_l, vp_l, ks_l, vs_l, cvp_l, cvs_l = [], [], [], [], [], []
    cp_l, cs_l, np_l, ns_l, mp_l, ms_l = [], [], [], [], [], []
    for l in range(DEPTH):
        wl = (w_in[l], conv_w[l], conv_b[l], conv_ln_g[l], conv_ln_b[l], mlstm_b_i[l], mlstm_b_f[l],
              mlstm_norm_g[l], w_out[l], ln1_g[l], ln1_b[l], peer_wq[l], peer_subkeys[l], peer_u[l],
              peer_v[l], ln2_g[l], ln2_b[l])
        yp, kp, vp, cvp, cp, np_, mp = trunk_layer(yp, empty_kv, empty_kv, conv_zero, c_zero, n_zero, m_zero, *wl)
        ys, ks_, vs_, cvs, cs, ns, ms = trunk_layer(ys, cache_attn_k[l], cache_attn_v[l], state_conv[l],
                                                    state_mlstm_c[l], state_mlstm_n[l], state_mlstm_m[l], *wl)
        kp_l.append(kp[:, T - win_p:]); vp_l.append(vp[:, T - win_p:])
        ks_l.append(ks_); vs_l.append(vs_)
        cvp_l.append(cvp); cvs_l.append(cvs)
        cp_l.append(cp); cs_l.append(cs)
        np_l.append(np_); ns_l.append(ns)
        mp_l.append(mp); ms_l.append(ms)

    k_prompt = jnp.stack(kp_l)
    v_prompt = jnp.stack(vp_l)
    k_sample = jnp.stack(ks_l)
    v_sample = jnp.stack(vs_l)
    conv_prompt = jnp.stack(cvp_l)
    conv_sample = jnp.stack(cvs_l)
    c_prompt = jnp.stack(cp_l)
    c_sample = jnp.stack(cs_l)
    n_prompt = jnp.stack(np_l)
    n_sample = jnp.stack(ns_l)
    m_prompt = jnp.stack(mp_l)
    m_sample = jnp.stack(ms_l)
    return (yp, ys, k_prompt, v_prompt, k_sample, v_sample, conv_prompt, conv_sample,
            c_prompt, c_sample, n_prompt, n_sample, m_prompt, m_sample)
```

```python
import functools
import math

import jax
import jax.numpy as jnp
from jax import lax
from jax.experimental import pallas as pl
from jax.experimental.pallas import tpu as pltpu

F32 = jnp.float32
BF16 = jnp.bfloat16

LANES = 128
SUBLANES_BF16 = 16
VMEM_LIMIT_BYTES = 56 * 1024 * 1024

HEAD_DIM = 128
DILATED_PATTERNS = ((128, 1), (512, 4), (2048, 16))
MLSTM_CHUNK = 64
PEER_TOPK = 16
LN_EPS = 1e-5
SAMPLE_ROWS = 16
CONV_HALO = 32
MASKED = -1e30


def _params(*sem):
    return pltpu.CompilerParams(dimension_semantics=sem,
                                vmem_limit_bytes=VMEM_LIMIT_BYTES)


def _pick_tile(n, target, align):
    best = None
    for t in range(align, min(n, target) + 1, align):
        if n % t == 0:
            best = t
    assert best is not None, (n, target, align)
    return best


def _mm_kernel(a_ref, w_ref, o_ref):
    o_ref[...] = jnp.dot(a_ref[...], w_ref[...],
                         preferred_element_type=F32).astype(o_ref.dtype)


def _matmul(a, w, *, tn, tm, name):
    n, k = a.shape
    m = w.shape[1]
    return pl.pallas_call(
        _mm_kernel,
        grid=(n // tn, m // tm),
        in_specs=[pl.BlockSpec((tn, k), lambda i, j: (i, 0)),
                  pl.BlockSpec((k, tm), lambda i, j: (0, j))],
        out_specs=pl.BlockSpec((tn, tm), lambda i, j: (i, j)),
        out_shape=jax.ShapeDtypeStruct((n, m), F32),
        compiler_params=_params("parallel", "parallel"),
        name=name,
    )(a, w)


def _res_ln_kernel(x_ref, y_ref, g_ref, b_ref, o_ref, ob_ref, *, alpha):
    z = alpha * x_ref[...] + y_ref[...]
    mu = jnp.mean(z, axis=-1, keepdims=True)
    zc = z - mu
    var = jnp.mean(zc * zc, axis=-1, keepdims=True)
    out = zc * lax.rsqrt(var + LN_EPS) * g_ref[...] + b_ref[...]
    o_ref[...] = out
    ob_ref[...] = out.astype(BF16)


def _res_ln(x, y, g, b, *, alpha, tn, name):
    n, d = x.shape
    row = pl.BlockSpec((tn, d), lambda i: (i, 0))
    vec = pl.BlockSpec((1, d), lambda i: (0, 0))
    return pl.pallas_call(
        functools.partial(_res_ln_kernel, alpha=alpha),
        grid=(n // tn,),
        in_specs=[row, row, vec, vec],
        out_specs=[row, row],
        out_shape=[jax.ShapeDtypeStruct((n, d), F32),
                   jax.ShapeDtypeStruct((n, d), BF16)],
        compiler_params=_params("parallel"),
        name=name,
    )(x, y, g.reshape(1, d), b.reshape(1, d))


def _conv_kernel(a_ref, gate_ref, prev_ref, cw_ref, cb_ref, lg_ref, lb_ref,
                 y_ref, new_ref, ext_ref, *, tt, n_valid_last, width):
    t = pl.program_id(1)
    keep = width - 1

    @pl.when(t == 0)
    def _():
        ext_ref[0:CONV_HALO, :] = jnp.zeros((CONV_HALO, ext_ref.shape[1]), F32)
        ext_ref[CONV_HALO - keep:CONV_HALO, :] = prev_ref[0]

    ext_ref[CONV_HALO:CONV_HALO + tt, :] = a_ref[...] * jax.nn.sigmoid(gate_ref[...])
    acc = jnp.zeros((tt, ext_ref.shape[1]), F32)
    for w in range(width):
        lo = CONV_HALO - keep + w
        acc = acc + ext_ref[lo:lo + tt, :] * cw_ref[w:w + 1, :]
    y = acc + cb_ref[...]
    mu = jnp.mean(y, axis=-1, keepdims=True)
    yc = y - mu
    var = jnp.mean(yc * yc, axis=-1, keepdims=True)
    y = yc * lax.rsqrt(var + LN_EPS) * lg_ref[...] + lb_ref[...]
    y_ref[...] = (y * jax.nn.sigmoid(y)).astype(y_ref.dtype)

    @pl.when(t == pl.num_programs(1) - 1)
    def _():
        hi = CONV_HALO + n_valid_last
        new_ref[0] = ext_ref[hi - keep:hi, :]

    ext_ref[0:CONV_HALO, :] = ext_ref[tt:tt + CONV_HALO, :]


def _conv_group(proj, prev, cw, cb, lg, lb, *, row0, batches, rows, n_valid, name):
    width, ch = cw.shape
    tt = _pick_tile(rows, 256, 8)
    nt = rows // tt
    blk0 = row0 // tt
    assert row0 % tt == 0 and n_valid > rows - tt
    n_valid_last = n_valid - (nt - 1) * tt
    rowspec = lambda col: pl.BlockSpec((tt, ch), lambda b, t: (blk0 + b * nt + t, col))
    vec = pl.BlockSpec((1, ch), lambda b, t: (0, 0))
    state = pl.BlockSpec((1, width - 1, ch), lambda b, t: (b, 0, 0))
    return pl.pallas_call(
        functools.partial(_conv_kernel, tt=tt, n_valid_last=n_valid_last, width=width),
        grid=(batches, nt),
        in_specs=[rowspec(0), rowspec(1), state,
                  pl.BlockSpec((width, ch), lambda b, t: (0, 0)), vec, vec, vec],
        out_specs=[pl.BlockSpec((tt, ch), lambda b, t: (b * nt + t, 0)), state],
        out_shape=[jax.ShapeDtypeStruct((batches * rows, ch), BF16),
                   jax.ShapeDtypeStruct((batches, width - 1, ch), F32)],
        scratch_shapes=[pltpu.VMEM((CONV_HALO + tt + CONV_HALO, ch), F32)],
        compiler_params=_params("parallel", "arbitrary"),
        name=name,
    )(proj, proj, prev, cw, cb.reshape(1, ch), lg.reshape(1, ch), lb.reshape(1, ch))


def _pattern_count(delta):
    cnt = jnp.zeros(delta.shape, F32)
    for window, dil in DILATED_PATTERNS:
        hit = (delta <= window) & ((delta & (dil - 1)) == 0)
        cnt = cnt + jnp.where(hit, 1.0, 0.0)
    return jnp.where(delta >= 0, cnt, 0.0)


def _attn_prompt_kernel(q_ref, k_ref, v_ref, o_ref, *, qb, scale):
    t_len = q_ref.shape[0]
    kb = k_ref[...].astype(BF16)
    vb = v_ref[...].astype(BF16)
    for i in range(t_len // qb):
        hi = (i + 1) * qb
        q = q_ref[i * qb:hi, :].astype(BF16)
        s = lax.dot_general(q, kb[0:hi], (((1,), (1,)), ((), ())),
                            preferred_element_type=F32) * scale
        qpos = i * qb + lax.broadcasted_iota(jnp.int32, (qb, hi), 0)
        kpos = lax.broadcasted_iota(jnp.int32, (qb, hi), 1)
        cnt = _pattern_count(qpos - kpos)
        s = jnp.where(cnt > 0, s, MASKED)
        mx = jnp.max(s, axis=-1, keepdims=True)
        e = jnp.exp(s - mx) * cnt
        den = jnp.sum(e, axis=-1, keepdims=True)
        out = jnp.dot(e.astype(BF16), vb[0:hi], preferred_element_type=F32)
        o_ref[i * qb:hi, :] = (out / den).astype(o_ref.dtype)


def _attn_prompt(proj, *, batches, t_len, heads, col0, name):
    c0 = col0 // HEAD_DIM
    qb = _pick_tile(t_len, 256, 8)
    spec = lambda which: pl.BlockSpec(
        (t_len, HEAD_DIM), lambda b, h: (b, c0 + which * heads + h))
    return pl.pallas_call(
        functools.partial(_attn_prompt_kernel, qb=qb, scale=HEAD_DIM ** -0.5),
        grid=(batches, heads),
        in_specs=[spec(0), spec(1), spec(2)],
        out_specs=pl.BlockSpec((t_len, HEAD_DIM), lambda b, h: (b, h)),
        out_shape=jax.ShapeDtypeStruct((batches * t_len, heads * HEAD_DIM), BF16),
        compiler_params=_params("parallel", "parallel"),
        name=name,
    )(proj, proj, proj)


def _attn_sample_kernel(q_ref, k_ref, v_ref, kc_ref, vc_ref, o_ref, *, n_valid, scale):
    rows = q_ref.shape[0]
    n_prev = kc_ref.shape[1]
    q = q_ref[...].astype(BF16)
    nt = (((1,), (1,)), ((), ()))
    s_c = lax.dot_general(q, kc_ref[0].astype(BF16), nt, preferred_element_type=F32) * scale
    s_n = lax.dot_general(q, k_ref[...].astype(BF16), nt, preferred_element_type=F32) * scale
    qpos_c = n_prev + lax.broadcasted_iota(jnp.int32, (rows, n_prev), 0)
    cnt_c = _pattern_count(qpos_c - lax.broadcasted_iota(jnp.int32, (rows, n_prev), 1))
    tq = lax.broadcasted_iota(jnp.int32, (rows, rows), 0)
    tk = lax.broadcasted_iota(jnp.int32, (rows, rows), 1)
    cnt_n = jnp.where(tk < n_valid, _pattern_count(tq - tk), 0.0)
    s_c = jnp.where(cnt_c > 0, s_c, MASKED)
    s_n = jnp.where(cnt_n > 0, s_n, MASKED)
    mx = jnp.maximum(jnp.max(s_c, axis=-1, keepdims=True),
                     jnp.max(s_n, axis=-1, keepdims=True))
    e_c = jnp.exp(s_c - mx) * cnt_c
    e_n = jnp.exp(s_n - mx) * cnt_n
    den = jnp.sum(e_c, axis=-1, keepdims=True) + jnp.sum(e_n, axis=-1, keepdims=True)
    out = (jnp.dot(e_c.astype(BF16), vc_ref[0].astype(BF16), preferred_element_type=F32)
           + jnp.dot(e_n.astype(BF16), v_ref[...].astype(BF16), preferred_element_type=F32))
    o_ref[...] = (out / den).astype(o_ref.dtype)


def _attn_sample(proj, cache_k, cache_v, *, row0, batches, heads, col0, n_valid, name):
    c0 = col0 // HEAD_DIM
    r0 = row0 // SAMPLE_ROWS
    n_prev = cache_k.shape[1]
    new = lambda which: pl.BlockSpec(
        (SAMPLE_ROWS, HEAD_DIM), lambda b, h: (r0 + b, c0 + which * heads + h))
    cache = pl.BlockSpec((1, n_prev, HEAD_DIM), lambda b, h: (b, 0, h))
    return pl.pallas_call(
        functools.partial(_attn_sample_kernel, n_valid=n_valid, scale=HEAD_DIM ** -0.5),
        grid=(batches, heads),
        in_specs=[new(0), new(1), new(2), cache, cache],
        out_specs=pl.BlockSpec((SAMPLE_ROWS, HEAD_DIM), lambda b, h: (b, h)),
        out_shape=jax.ShapeDtypeStruct((batches * SAMPLE_ROWS, heads * HEAD_DIM), BF16),
        compiler_params=_params("parallel", "parallel"),
        name=name,
    )(proj, proj, proj,
      cache_k.reshape(batches, n_prev, heads * HEAD_DIM),
      cache_v.reshape(batches, n_prev, heads * HEAD_DIM))


def _mlstm_kernel(q_ref, k_ref, v_ref, og_ref, gi_ref, gf_ref, bi_ref, bf_ref, ng_ref,
                  c0_ref, n0_ref, m0_ref, y_ref, c_ref, n_ref, m_ref,
                  *, chunk, n_chunks, n_valid):
    L = chunk
    row = lax.broadcasted_iota(jnp.int32, (L, L), 0)
    col = lax.broadcasted_iota(jnp.int32, (L, L), 1)
    causal = col <= row
    eye = jnp.where(row == col, 1.0, 0.0)
    upper = jnp.where(row <= col, 1.0, 0.0)
    lower = jnp.where(causal, 1.0, 0.0)
    lane = lax.broadcasted_iota(jnp.int32, (1, L), 1)
    tn = (((0,), (0,)), ((), ()))
    nt = (((1,), (1,)), ((), ()))

    def to_col(r):
        return jnp.sum(eye * r, axis=-1, keepdims=True)

    def body(c, carry):
        cs, ns, ms = carry
        r0 = pl.multiple_of(c * L, L)
        q = q_ref[pl.ds(r0, L), :]
        k = k_ref[pl.ds(r0, L), :] * (HEAD_DIM ** -0.5)
        v = v_ref[pl.ds(r0, L), :]
        i_row = gi_ref[0, 0, pl.ds(c, 1), :] + bi_ref[0]
        f_pre = gf_ref[0, 0, pl.ds(c, 1), :] + bf_ref[0]
        f_row = jnp.minimum(f_pre, 0.0) - jnp.log(1.0 + jnp.exp(-jnp.abs(f_pre)))
        if n_valid < L:
            i_row = jnp.where(lane < n_valid, i_row, MASKED)
            f_row = jnp.where(lane < n_valid, f_row, 0.0)
        i_col = to_col(i_row)
        f_col = to_col(f_row)
        b_col = jnp.sum(lower * f_row, axis=-1, keepdims=True)
        b_row = jnp.sum(upper * f_col, axis=0, keepdims=True)
        a_intra = jnp.where(causal, b_col - b_row + i_row, MASKED)
        a_inter = b_col + ms
        m_t = jnp.maximum(a_inter, jnp.max(a_intra, axis=-1, keepdims=True))
        w_intra = jnp.exp(a_intra - m_t)
        w_inter = jnp.exp(a_inter - m_t)
        qb = q.astype(BF16)
        kb = k.astype(BF16)
        vb = v.astype(BF16)
        qk = lax.dot_general(qb, kb, nt, preferred_element_type=F32) * w_intra
        num = (w_inter * jnp.dot(qb, cs.astype(BF16), preferred_element_type=F32)
               + jnp.dot(qk.astype(BF16), vb, preferred_element_type=F32))
        qn = (w_inter * jnp.sum(q * ns, axis=-1, keepdims=True)
              + jnp.sum(qk, axis=-1, keepdims=True))
        h = num / jnp.maximum(jnp.abs(qn), jnp.exp(-m_t))
        m_new = m_t[L - 1:L, :]
        b_last = b_col[L - 1:L, :]
        decay = jnp.exp(b_last + ms - m_new)
        g_col = jnp.exp(b_last - b_col + i_col - m_new)
        kg = k * g_col
        c_new = decay * cs + lax.dot_general(kg.astype(BF16), vb, tn,
                                             preferred_element_type=F32)
        n_new = decay * ns + jnp.sum(kg, axis=0, keepdims=True)
        hg = jax.nn.sigmoid(og_ref[pl.ds(r0, L), :]) * h
        mu = jnp.mean(hg, axis=-1, keepdims=True)
        hc = hg - mu
        var = jnp.mean(hc * hc, axis=-1, keepdims=True)
        y_ref[pl.ds(r0, L), :] = (hc * lax.rsqrt(var + LN_EPS) * ng_ref[0]).astype(y_ref.dtype)
        return c_new, n_new, m_new

    cs, ns, ms = lax.fori_loop(0, n_chunks, body,
                               (c0_ref[0, 0], n0_ref[0, 0], m0_ref[0, 0]))
    c_ref[0, 0] = cs
    n_ref[0, 0] = ns
    m_ref[0, 0] = ms


def _mlstm_group(proj, gates_t, b_i, b_f, norm_g, c0, n0, m0,
                 *, row0, batches, rows, n_valid, col0, name):
    heads = b_i.shape[0]
    chunk = MLSTM_CHUNK if n_valid % MLSTM_CHUNK == 0 else rows
    assert n_valid == rows or chunk == rows
    n_chunks = rows // chunk
    c0b = col0 // HEAD_DIM
    rb = row0 // rows
    g4 = gates_t[:, row0:row0 + batches * rows].reshape(2 * heads, batches, n_chunks, chunk)
    tok = lambda which: pl.BlockSpec(
        (rows, HEAD_DIM), lambda b, h: (rb + b, c0b + which * heads + h))
    gate = lambda off: pl.BlockSpec(
        (1, 1, n_chunks, chunk), lambda b, h: (off + h, b, 0, 0))
    scal = pl.BlockSpec((1, 1, 1), lambda b, h: (h, 0, 0))
    cspec = pl.BlockSpec((1, 1, HEAD_DIM, HEAD_DIM), lambda b, h: (b, h, 0, 0))
    nspec = pl.BlockSpec((1, 1, 1, HEAD_DIM), lambda b, h: (b, h, 0, 0))
    mspec = pl.BlockSpec((1, 1, 1, 1), lambda b, h: (b, h, 0, 0))
    y, c, n, m = pl.pallas_call(
        functools.partial(_mlstm_kernel, chunk=chunk, n_chunks=n_chunks, n_valid=min(n_valid, chunk)),
        grid=(batches, heads),
        in_specs=[tok(0), tok(1), tok(2), tok(3), gate(0), gate(heads), scal, scal,
                  pl.BlockSpec((1, 1, HEAD_DIM), lambda b, h: (h, 0, 0)),
                  cspec, nspec, mspec],
        out_specs=[pl.BlockSpec((rows, HEAD_DIM), lambda b, h: (b, h)), cspec, nspec, mspec],
        out_shape=[jax.ShapeDtypeStruct((batches * rows, heads * HEAD_DIM), BF16),
                   jax.ShapeDtypeStruct(c0.shape, F32),
                   jax.ShapeDtypeStruct(n0.shape[:2] + (1, HEAD_DIM), F32),
                   jax.ShapeDtypeStruct(m0.shape + (1, 1), F32)],
        compiler_params=_params("parallel", "parallel"),
        name=name,
    )(proj, proj, proj, proj, g4, g4,
      b_i.reshape(heads, 1, 1), b_f.reshape(heads, 1, 1), norm_g.reshape(heads, 1, HEAD_DIM),
      c0, n0.reshape(n0.shape[:2] + (1, HEAD_DIM)), m0.reshape(m0.shape + (1, 1)))
    return y, c, n.reshape(n0.shape), m.reshape(m0.shape)


def _topk_rows(vals, payload, k):
    rows = vals.shape[0]
    ridx = lax.broadcasted_iota(jnp.int32, vals.shape, 0).astype(F32)
    out_v, out_p = [], []
    for _ in range(k):
        m = jnp.max(vals, axis=0, keepdims=True)
        first = jnp.min(jnp.where(vals == m, ridx, float(rows)), axis=0, keepdims=True)
        hit = ridx == first
        out_v.append(m)
        out_p.append(jnp.sum(jnp.where(hit, payload, 0.0), axis=0, keepdims=True))
        vals = jnp.where(hit, -jnp.inf, vals)
    return jnp.concatenate(out_v, axis=0), jnp.concatenate(out_p, axis=0)


def _route_kernel(q_ref, sk_ref, g_ref, it_ref, jt_ref, gt_ref, *, heads, n_keys):
    tok = q_ref.shape[0]
    half = sk_ref.shape[-1]
    K = PEER_TOPK
    nt = (((1,), (1,)), ((), ()))
    key_id = lax.broadcasted_iota(jnp.int32, (n_keys, tok), 0).astype(F32)
    i_rows, j_rows, g_rows = [], [], []
    for h in range(heads):
        sv, si = [], []
        for p in range(2):
            c = (h * 2 + p) * half
            qh = q_ref[:, c:c + half].astype(BF16)
            st = lax.dot_general(sk_ref[h, p], qh, nt, preferred_element_type=F32)
            v, i = _topk_rows(st, key_id, K)
            sv.append(v)
            si.append(i)
        cand = jnp.concatenate([sv[0][a:a + 1] + sv[1] for a in range(K)], axis=0)
        cid = jnp.concatenate([si[0][a:a + 1] * n_keys + si[1] for a in range(K)], axis=0)
        cv, eid = _topk_rows(cand, cid, K)
        e = jnp.exp(cv - cv[0:1])
        gate = e / jnp.sum(e, axis=0, keepdims=True)
        ei = jnp.floor(eid * (1.0 / n_keys))
        i_rows.append(ei)
        j_rows.append(eid - ei * n_keys)
        g_rows.append(gate)
    it_ref[...] = jnp.concatenate(i_rows, axis=0).T
    jt_ref[...] = jnp.concatenate(j_rows, axis=0).T
    gt_ref[...] = jnp.concatenate(g_rows, axis=0).T

    picks = heads * K
    sub = lax.broadcasted_iota(jnp.int32, (n_keys, picks), 0).astype(F32)

    def per_token(n, _):
        irow = it_ref[pl.ds(n, 1), :]
        jrow = jt_ref[pl.ds(n, 1), :]
        grow = gt_ref[pl.ds(n, 1), :]
        a_t = jnp.where(sub == irow, 1.0, 0.0).astype(BF16)
        b_t = jnp.where(sub == jrow, grow, 0.0).astype(BF16)
        g_ref[n] = lax.dot_general(a_t, b_t, nt, preferred_element_type=F32).astype(g_ref.dtype)
        return 0

    lax.fori_loop(0, tok, per_token, 0)


def _route(qp, subkeys_bf, *, name):
    n = qp.shape[0]
    heads, _, n_keys, half = subkeys_bf.shape
    tok = LANES
    picks = heads * PEER_TOPK
    g3 = pl.pallas_call(
        functools.partial(_route_kernel, heads=heads, n_keys=n_keys),
        grid=(n // tok,),
        in_specs=[pl.BlockSpec((tok, qp.shape[1]), lambda i: (i, 0)),
                  pl.BlockSpec(subkeys_bf.shape, lambda i: (0, 0, 0, 0))],
        out_specs=pl.BlockSpec((tok, n_keys, n_keys), lambda i: (i, 0, 0)),
        out_shape=jax.ShapeDtypeStruct((n, n_keys, n_keys), BF16),
        scratch_shapes=[pltpu.VMEM((tok, picks), F32)] * 3,
        compiler_params=_params("parallel"),
        name=name,
    )(qp, subkeys_bf)
    return g3.reshape(n, n_keys * n_keys)


def _peer_kernel(x_ref, ut_ref, v_ref, g_ref, o_ref):
    s = jnp.dot(x_ref[...], ut_ref[...], preferred_element_type=F32)
    act = 0.5 * s * (1.0 + lax.erf(s * (1.0 / math.sqrt(2.0))))
    p = (act * g_ref[...].astype(F32)).astype(BF16)
    contrib = jnp.dot(p, v_ref[...], preferred_element_type=F32)

    @pl.when(pl.program_id(1) == 0)
    def _():
        o_ref[...] = contrib

    @pl.when(pl.program_id(1) > 0)
    def _():
        o_ref[...] += contrib


def _peer_dense(xb, ut, v, g, *, tn, te, name):
    n, d = xb.shape
    n_exp = v.shape[0]
    return pl.pallas_call(
        _peer_kernel,
        grid=(n // tn, n_exp // te),
        in_specs=[pl.BlockSpec((tn, d), lambda i, e: (i, 0)),
                  pl.BlockSpec((d, te), lambda i, e: (0, e)),
                  pl.BlockSpec((te, d), lambda i, e: (e, 0)),
                  pl.BlockSpec((tn, te), lambda i, e: (i, e))],
        out_specs=pl.BlockSpec((tn, d), lambda i, e: (i, 0)),
        out_shape=jax.ShapeDtypeStruct((n, d), F32),
        compiler_params=_params("parallel", "arbitrary"),
        name=name,
    )(xb, ut, v, g)


def _layer(x, xb, st, w, *, dims, alpha, tag):
    bp, tp, bs, ts = dims
    n, d = x.shape
    rows_p = bp * tp
    conv_ch = w["conv_w"].shape[-1]
    a_heads = st["cache_k"].shape[2]
    m_heads = w["b_i"].shape[0]
    attn_w = a_heads * HEAD_DIM
    mlstm_w = m_heads * HEAD_DIM
    off_att = 2 * conv_ch
    off_mls = off_att + 3 * attn_w
    off_gate = off_mls + 4 * mlstm_w

    tn_big = _pick_tile(n, 1664, SUBLANES_BF16)
    w_in = w["w_in"].astype(BF16)
    proj = _matmul(xb, w_in[:, :off_gate], tn=tn_big,
                   tm=_pick_tile(off_gate, 512, LANES), name=f"proj{tag}")
    w_gate = jnp.pad(w_in[:, off_gate:], ((0, 0), (0, LANES - 2 * m_heads)))
    gates = _matmul(xb, w_gate, tn=tn_big, tm=LANES, name=f"gates{tag}")
    gates_t = gates[:, :2 * m_heads].T

    zeros = lambda *s: jnp.zeros(s, F32)
    conv_args = (w["conv_w"], w["conv_b"], w["conv_ln_g"], w["conv_ln_b"])
    ya_p, conv_p = _conv_group(proj, zeros(bp, w["conv_w"].shape[0] - 1, conv_ch), *conv_args,
                               row0=0, batches=bp, rows=tp, n_valid=tp, name=f"conv_p{tag}")
    ya_s, conv_s = _conv_group(proj, st["conv"], *conv_args, row0=rows_p, batches=bs,
                               rows=SAMPLE_ROWS, n_valid=ts, name=f"conv_s{tag}")

    yb_p = _attn_prompt(proj, batches=bp, t_len=tp, heads=a_heads, col0=off_att,
                        name=f"attn_p{tag}")
    yb_s = _attn_sample(proj, st["cache_k"], st["cache_v"], row0=rows_p, batches=bs,
                        heads=a_heads, col0=off_att, n_valid=ts, name=f"attn_s{tag}")

    ml_args = (w["b_i"], w["b_f"], w["norm_g"])
    yc_p, c_p, n_p, m_p = _mlstm_group(
        proj, gates_t, *ml_args, zeros(bp, m_heads, HEAD_DIM, HEAD_DIM),
        zeros(bp, m_heads, HEAD_DIM), zeros(bp, m_heads),
        row0=0, batches=bp, rows=tp, n_valid=tp, col0=off_mls, name=f"mlstm_p{tag}")
    yc_s, c_s, n_s, m_s = _mlstm_group(
        proj, gates_t, *ml_args, st["c"], st["n"], st["m"],
        row0=rows_p, batches=bs, rows=SAMPLE_ROWS, n_valid=ts, col0=off_mls,
        name=f"mlstm_s{tag}")

    rows_s = bs * SAMPLE_ROWS
    ymix = jnp.concatenate([
        jnp.concatenate([ya_p, yb_p, yc_p], axis=1),
        jnp.concatenate([ya_s, yb_s, yc_s], axis=1),
        jnp.zeros((n - rows_p - rows_s, d), BF16)], axis=0)
    mix = _matmul(ymix, w["w_out"].astype(BF16), tn=tn_big,
                  tm=_pick_tile(d, 512, LANES), name=f"outproj{tag}")
    tn_ln = _pick_tile(n, 256, SUBLANES_BF16)
    x1, x1b = _res_ln(x, mix, w["ln1_g"], w["ln1_b"], alpha=alpha, tn=tn_ln, name=f"ln1{tag}")

    qp = _matmul(x1b, w["peer_wq"].astype(BF16), tn=tn_big,
                 tm=_pick_tile(w["peer_wq"].shape[1], 512, LANES), name=f"peer_q{tag}")
    g = _route(qp, w["peer_subkeys"].astype(BF16), name=f"route{tag}")
    n_exp = w["peer_u"].shape[0]
    po = _peer_dense(x1b, w["peer_u"].astype(BF16).T, w["peer_v"].astype(BF16), g,
                     tn=_pick_tile(n, 640, SUBLANES_BF16), te=_pick_tile(n_exp, 256, LANES),
                     name=f"peer{tag}")
    x2, x2b = _res_ln(x1, po, w["ln2_g"], w["ln2_b"], alpha=alpha, tn=tn_ln, name=f"ln2{tag}")

    win = min(max(wd for wd, _ in DILATED_PATTERNS), tp)
    kv = lambda rows, b, t, which: rows[:, off_att + which * attn_w:off_att + (which + 1) * attn_w
                                        ].reshape(b, t, a_heads, HEAD_DIM)
    pr = proj[:rows_p]
    sr = proj[rows_p:rows_p + rows_s]
    outs = dict(
        k_p=kv(pr, bp, tp, 1)[:, tp - win:], v_p=kv(pr, bp, tp, 2)[:, tp - win:],
        k_s=kv(sr, bs, SAMPLE_ROWS, 1)[:, :ts], v_s=kv(sr, bs, SAMPLE_ROWS, 2)[:, :ts],
        conv_p=conv_p, conv_s=conv_s, c_p=c_p, c_s=c_s, n_p=n_p, n_s=n_s, m_p=m_p, m_s=m_s)
    return x2, x2b, outs


def kernel(x_prompt, x_sample, cache_attn_k, cache_attn_v, state_conv, state_mlstm_c, state_mlstm_n, state_mlstm_m, w_in, conv_w, conv_b, conv_ln_g, conv_ln_b, mlstm_b_i, mlstm_b_f, mlstm_norm_g, w_out, ln1_g, ln1_b, peer_wq, peer_subkeys, peer_u, peer_v, ln2_g, ln2_b):
    bp, tp, d = x_prompt.shape
    bs, ts, _ = x_sample.shape
    depth = w_in.shape[0]
    assert ts <= SAMPLE_ROWS and tp % SAMPLE_ROWS == 0
    rows_p = bp * tp
    rows_s = bs * SAMPLE_ROWS
    n = -(-(rows_p + rows_s) // LANES) * LANES
    xs = jnp.pad(x_sample, ((0, 0), (0, SAMPLE_ROWS - ts), (0, 0))).reshape(rows_s, d)
    x = jnp.concatenate([x_prompt.reshape(rows_p, d), xs,
                         jnp.zeros((n - rows_p - rows_s, d), x_prompt.dtype)], axis=0)
    xb = x.astype(BF16)
    alpha = (2 * depth) ** 0.25

    per_layer = []
    for l in range(depth):
        st = dict(cache_k=cache_attn_k[l], cache_v=cache_attn_v[l], conv=state_conv[l],
                  c=state_mlstm_c[l], n=state_mlstm_n[l], m=state_mlstm_m[l])
        w = dict(w_in=w_in[l], conv_w=conv_w[l], conv_b=conv_b[l], conv_ln_g=conv_ln_g[l],
                 conv_ln_b=conv_ln_b[l], b_i=mlstm_b_i[l], b_f=mlstm_b_f[l],
                 norm_g=mlstm_norm_g[l], w_out=w_out[l], ln1_g=ln1_g[l], ln1_b=ln1_b[l],
                 peer_wq=peer_wq[l], peer_subkeys=peer_subkeys[l], peer_u=peer_u[l],
                 peer_v=peer_v[l], ln2_g=ln2_g[l], ln2_b=ln2_b[l])
        x, xb, outs = _layer(x, xb, st, w, dims=(bp, tp, bs, ts), alpha=alpha, tag=str(l))
        per_layer.append(outs)

    stack = lambda key: jnp.stack([o[key] for o in per_layer])
    y_prompt = x[:rows_p].reshape(bp, tp, d)
    y_sample = x[rows_p:rows_p + rows_s].reshape(bs, SAMPLE_ROWS, d)[:, :ts]
    return (y_prompt, y_sample, stack("k_p"), stack("v_p"), stack("k_s"), stack("v_s"),
            stack("conv_p"), stack("conv_s"), stack("c_p"), stack("c_s"),
            stack("n_p"), stack("n_s"), stack("m_p"), stack("m_s"))
```

```python
import functools
import math

import jax
import jax.numpy as jnp
from jax import lax
from jax.experimental import pallas as pl
from jax.experimental.pallas import tpu as pltpu

F32 = jnp.float32
BF16 = jnp.bfloat16

LANES = 128
SUBLANES_BF16 = 16
VMEM_LIMIT_BYTES = 56 * 1024 * 1024

HEAD_DIM = 128
DILATED_PATTERNS = ((128, 1), (512, 4), (2048, 16))
MLSTM_CHUNK = 64
PEER_TOPK = 16
LN_EPS = 1e-5
SAMPLE_ROWS = 16
CONV_HALO = 32
MASKED = -1e30
ROUTE_UNROLL = 8


def _params(*sem):
    return pltpu.CompilerParams(dimension_semantics=sem,
                                vmem_limit_bytes=VMEM_LIMIT_BYTES)


def _pick_tile(n, target, align):
    best = None
    for t in range(align, min(n, target) + 1, align):
        if n % t == 0:
            best = t
    assert best is not None, (n, target, align)
    return best


def _mm_kernel(a_ref, w_ref, o_ref):
    o_ref[...] = jnp.dot(a_ref[...], w_ref[...],
                         preferred_element_type=F32).astype(o_ref.dtype)


def _matmul(a, w, *, tn, tm, name):
    n, k = a.shape
    m = w.shape[1]
    return pl.pallas_call(
        _mm_kernel,
        grid=(n // tn, m // tm),
        in_specs=[pl.BlockSpec((tn, k), lambda i, j: (i, 0)),
                  pl.BlockSpec((k, tm), lambda i, j: (0, j))],
        out_specs=pl.BlockSpec((tn, tm), lambda i, j: (i, j)),
        out_shape=jax.ShapeDtypeStruct((n, m), F32),
        compiler_params=_params("parallel", "parallel"),
        name=name,
    )(a, w)


def _mm_acc_kernel(a_ref, w_ref, o_ref, acc_ref):
    k = pl.program_id(2)
    part = jnp.dot(a_ref[...], w_ref[...], preferred_element_type=F32)

    @pl.when(k == 0)
    def _():
        acc_ref[...] = part

    @pl.when(k > 0)
    def _():
        acc_ref[...] += part

    @pl.when(k == pl.num_programs(2) - 1)
    def _():
        o_ref[...] = acc_ref[...]


def _matmul_ktiled(a, w, *, tn, tm, tk, name):
    n, k = a.shape
    m = w.shape[1]
    return pl.pallas_call(
        _mm_acc_kernel,
        grid=(n // tn, m // tm, k // tk),
        in_specs=[pl.BlockSpec((tn, tk), lambda i, j, q: (i, q)),
                  pl.BlockSpec((tk, tm), lambda i, j, q: (q, j))],
        out_specs=pl.BlockSpec((tn, tm), lambda i, j, q: (i, j)),
        out_shape=jax.ShapeDtypeStruct((n, m), F32),
        scratch_shapes=[pltpu.VMEM((tn, tm), F32)],
        compiler_params=_params("parallel", "parallel", "arbitrary"),
        name=name,
    )(a, w)


def _res_ln_kernel(x_ref, y_ref, g_ref, b_ref, o_ref, ob_ref, *, alpha):
    z = alpha * x_ref[...] + y_ref[...]
    mu = jnp.mean(z, axis=-1, keepdims=True)
    zc = z - mu
    var = jnp.mean(zc * zc, axis=-1, keepdims=True)
    out = zc * lax.rsqrt(var + LN_EPS) * g_ref[...] + b_ref[...]
    o_ref[...] = out
    ob_ref[...] = out.astype(BF16)


def _res_ln(x, y, g, b, *, alpha, tn, name):
    n, d = x.shape
    row = pl.BlockSpec((tn, d), lambda i: (i, 0))
    vec = pl.BlockSpec((1, d), lambda i: (0, 0))
    return pl.pallas_call(
        functools.partial(_res_ln_kernel, alpha=alpha),
        grid=(n // tn,),
        in_specs=[row, row, vec, vec],
        out_specs=[row, row],
        out_shape=[jax.ShapeDtypeStruct((n, d), F32),
                   jax.ShapeDtypeStruct((n, d), BF16)],
        compiler_params=_params("parallel"),
        name=name,
    )(x, y, g.reshape(1, d), b.reshape(1, d))


def _conv_kernel(a_ref, gate_ref, prev_ref, cw_ref, cb_ref, lg_ref, lb_ref,
                 y_ref, new_ref, ext_ref, *, tt, n_valid_last, width):
    t = pl.program_id(1)
    keep = width - 1

    @pl.when(t == 0)
    def _():
        ext_ref[0:CONV_HALO, :] = jnp.zeros((CONV_HALO, ext_ref.shape[1]), F32)
        ext_ref[CONV_HALO - keep:CONV_HALO, :] = prev_ref[0]

    ext_ref[CONV_HALO:CONV_HALO + tt, :] = a_ref[...] * jax.nn.sigmoid(gate_ref[...])
    acc = jnp.zeros((tt, ext_ref.shape[1]), F32)
    for w in range(width):
        lo = CONV_HALO - keep + w
        acc = acc + ext_ref[lo:lo + tt, :] * cw_ref[w:w + 1, :]
    y = acc + cb_ref[...]
    mu = jnp.mean(y, axis=-1, keepdims=True)
    yc = y - mu
    var = jnp.mean(yc * yc, axis=-1, keepdims=True)
    y = yc * lax.rsqrt(var + LN_EPS) * lg_ref[...] + lb_ref[...]
    y_ref[...] = (y * jax.nn.sigmoid(y)).astype(y_ref.dtype)

    @pl.when(t == pl.num_programs(1) - 1)
    def _():
        hi = CONV_HALO + n_valid_last
        new_ref[0] = ext_ref[hi - keep:hi, :]

    ext_ref[0:CONV_HALO, :] = ext_ref[tt:tt + CONV_HALO, :]


def _conv_group(proj, prev, cw, cb, lg, lb, *, row0, batches, rows, n_valid, name):
    width, ch = cw.shape
    tt = _pick_tile(rows, 256, 8)
    nt = rows // tt
    blk0 = row0 // tt
    assert row0 % tt == 0 and n_valid > rows - tt
    n_valid_last = n_valid - (nt - 1) * tt
    rowspec = lambda col: pl.BlockSpec((tt, ch), lambda b, t: (blk0 + b * nt + t, col))
    vec = pl.BlockSpec((1, ch), lambda b, t: (0, 0))
    state = pl.BlockSpec((1, width - 1, ch), lambda b, t: (b, 0, 0))
    return pl.pallas_call(
        functools.partial(_conv_kernel, tt=tt, n_valid_last=n_valid_last, width=width),
        grid=(batches, nt),
        in_specs=[rowspec(0), rowspec(1), state,
                  pl.BlockSpec((width, ch), lambda b, t: (0, 0)), vec, vec, vec],
        out_specs=[pl.BlockSpec((tt, ch), lambda b, t: (b * nt + t, 0)), state],
        out_shape=[jax.ShapeDtypeStruct((batches * rows, ch), BF16),
                   jax.ShapeDtypeStruct((batches, width - 1, ch), F32)],
        scratch_shapes=[pltpu.VMEM((CONV_HALO + tt + CONV_HALO, ch), F32)],
        compiler_params=_params("parallel", "arbitrary"),
        name=name,
    )(proj, proj, prev, cw, cb.reshape(1, ch), lg.reshape(1, ch), lb.reshape(1, ch))


def _pattern_count(delta):
    cnt = jnp.zeros(delta.shape, F32)
    for window, dil in DILATED_PATTERNS:
        hit = (delta <= window) & ((delta & (dil - 1)) == 0)
        cnt = cnt + jnp.where(hit, 1.0, 0.0)
    return jnp.where(delta >= 0, cnt, 0.0)


def _attn_prompt_kernel(q_ref, k_ref, v_ref, o_ref, *, qb, scale):
    t_len = q_ref.shape[0]
    kb = k_ref[...].astype(BF16)
    vb = v_ref[...].astype(BF16)
    for i in range(t_len // qb):
        hi = (i + 1) * qb
        q = q_ref[i * qb:hi, :].astype(BF16)
        s = lax.dot_general(q, kb[0:hi], (((1,), (1,)), ((), ())),
                            preferred_element_type=F32) * scale
        qpos = i * qb + lax.broadcasted_iota(jnp.int32, (qb, hi), 0)
        kpos = lax.broadcasted_iota(jnp.int32, (qb, hi), 1)
        cnt = _pattern_count(qpos - kpos)
        s = jnp.where(cnt > 0, s, MASKED)
        mx = jnp.max(s, axis=-1, keepdims=True)
        e = jnp.exp(s - mx) * cnt
        den = jnp.sum(e, axis=-1, keepdims=True)
        out = jnp.dot(e.astype(BF16), vb[0:hi], preferred_element_type=F32)
        o_ref[i * qb:hi, :] = (out / den).astype(o_ref.dtype)


def _attn_prompt(proj, *, batches, t_len, heads, col0, name):
    c0 = col0 // HEAD_DIM
    qb = _pick_tile(t_len, 256, 8)
    spec = lambda which: pl.BlockSpec(
        (t_len, HEAD_DIM), lambda b, h: (b, c0 + which * heads + h))
    return pl.pallas_call(
        functools.partial(_attn_prompt_kernel, qb=qb, scale=HEAD_DIM ** -0.5),
        grid=(batches, heads),
        in_specs=[spec(0), spec(1), spec(2)],
        out_specs=pl.BlockSpec((t_len, HEAD_DIM), lambda b, h: (b, h)),
        out_shape=jax.ShapeDtypeStruct((batches * t_len, heads * HEAD_DIM), BF16),
        compiler_params=_params("parallel", "parallel"),
        name=name,
    )(proj, proj, proj)


def _attn_sample_kernel(q_ref, k_ref, v_ref, kc_ref, vc_ref, o_ref, *, n_valid, scale):
    rows = q_ref.shape[0]
    n_prev = kc_ref.shape[1]
    q = q_ref[...].astype(BF16)
    nt = (((1,), (1,)), ((), ()))
    s_c = lax.dot_general(q, kc_ref[0].astype(BF16), nt, preferred_element_type=F32) * scale
    s_n = lax.dot_general(q, k_ref[...].astype(BF16), nt, preferred_element_type=F32) * scale
    qpos_c = n_prev + lax.broadcasted_iota(jnp.int32, (rows, n_prev), 0)
    cnt_c = _pattern_count(qpos_c - lax.broadcasted_iota(jnp.int32, (rows, n_prev), 1))
    tq = lax.broadcasted_iota(jnp.int32, (rows, rows), 0)
    tk = lax.broadcasted_iota(jnp.int32, (rows, rows), 1)
    cnt_n = jnp.where(tk < n_valid, _pattern_count(tq - tk), 0.0)
    s_c = jnp.where(cnt_c > 0, s_c, MASKED)
    s_n = jnp.where(cnt_n > 0, s_n, MASKED)
    mx = jnp.maximum(jnp.max(s_c, axis=-1, keepdims=True),
                     jnp.max(s_n, axis=-1, keepdims=True))
    e_c = jnp.exp(s_c - mx) * cnt_c
    e_n = jnp.exp(s_n - mx) * cnt_n
    den = jnp.sum(e_c, axis=-1, keepdims=True) + jnp.sum(e_n, axis=-1, keepdims=True)
    out = (jnp.dot(e_c.astype(BF16), vc_ref[0].astype(BF16), preferred_element_type=F32)
           + jnp.dot(e_n.astype(BF16), v_ref[...].astype(BF16), preferred_element_type=F32))
    o_ref[...] = (out / den).astype(o_ref.dtype)


def _attn_sample(proj, cache_k, cache_v, *, row0, batches, heads, col0, n_valid, name):
    c0 = col0 // HEAD_DIM
    r0 = row0 // SAMPLE_ROWS
    n_prev = cache_k.shape[1]
    new = lambda which: pl.BlockSpec(
        (SAMPLE_ROWS, HEAD_DIM), lambda b, h: (r0 + b, c0 + which * heads + h))
    cache = pl.BlockSpec((1, n_prev, HEAD_DIM), lambda b, h: (b, 0, h))
    return pl.pallas_call(
        functools.partial(_attn_sample_kernel, n_valid=n_valid, scale=HEAD_DIM ** -0.5),
        grid=(batches, heads),
        in_specs=[new(0), new(1), new(2), cache, cache],
        out_specs=pl.BlockSpec((SAMPLE_ROWS, HEAD_DIM), lambda b, h: (b, h)),
        out_shape=jax.ShapeDtypeStruct((batches * SAMPLE_ROWS, heads * HEAD_DIM), BF16),
        compiler_params=_params("parallel", "parallel"),
        name=name,
    )(proj, proj, proj,
      cache_k.reshape(batches, n_prev, heads * HEAD_DIM),
      cache_v.reshape(batches, n_prev, heads * HEAD_DIM))


def _mlstm_kernel(*refs, chunk, n_chunks, n_valid, hb):
    tok_refs = refs[:4 * hb]
    (gi_ref, gf_ref, bi_ref, bf_ref, ng_ref, c0_ref, n0_ref, m0_ref,
     y_ref, c_ref, n_ref, m_ref) = refs[4 * hb:]
    L = chunk
    row = lax.broadcasted_iota(jnp.int32, (L, L), 0)
    col = lax.broadcasted_iota(jnp.int32, (L, L), 1)
    causal = col <= row
    eye = jnp.where(row == col, 1.0, 0.0)
    upper = jnp.where(row <= col, 1.0, 0.0)
    lower = jnp.where(causal, 1.0, 0.0)
    lane = lax.broadcasted_iota(jnp.int32, (1, L), 1)
    tn = (((0,), (0,)), ((), ()))
    nt = (((1,), (1,)), ((), ()))

    def to_col(r):
        return jnp.sum(eye * r, axis=-1, keepdims=True)

    def one_head(j, c, r0, cs, ns, ms):
        q_ref, k_ref, v_ref, og_ref = tok_refs[4 * j:4 * j + 4]
        q = q_ref[pl.ds(r0, L), :]
        k = k_ref[pl.ds(r0, L), :] * (HEAD_DIM ** -0.5)
        v = v_ref[pl.ds(r0, L), :]
        i_row = gi_ref[j, 0, pl.ds(c, 1), :] + bi_ref[j]
        f_pre = gf_ref[j, 0, pl.ds(c, 1), :] + bf_ref[j]
        f_row = jnp.minimum(f_pre, 0.0) - jnp.log(1.0 + jnp.exp(-jnp.abs(f_pre)))
        if n_valid < L:
            i_row = jnp.where(lane < n_valid, i_row, MASKED)
            f_row = jnp.where(lane < n_valid, f_row, 0.0)
        i_col = to_col(i_row)
        f_col = to_col(f_row)
        b_col = jnp.sum(lower * f_row, axis=-1, keepdims=True)
        b_row = jnp.sum(upper * f_col, axis=0, keepdims=True)
        a_intra = jnp.where(causal, b_col - b_row + i_row, MASKED)
        a_inter = b_col + ms
        m_t = jnp.maximum(a_inter, jnp.max(a_intra, axis=-1, keepdims=True))
        w_intra = jnp.exp(a_intra - m_t)
        w_inter = jnp.exp(a_inter - m_t)
        qb = q.astype(BF16)
        kb = k.astype(BF16)
        vb = v.astype(BF16)
        qk = lax.dot_general(qb, kb, nt, preferred_element_type=F32) * w_intra
        num = (w_inter * jnp.dot(qb, cs.astype(BF16), preferred_element_type=F32)
               + jnp.dot(qk.astype(BF16), vb, preferred_element_type=F32))
        qn = (w_inter * jnp.sum(q * ns, axis=-1, keepdims=True)
              + jnp.sum(qk, axis=-1, keepdims=True))
        h = num / jnp.maximum(jnp.abs(qn), jnp.exp(-m_t))
        m_new = m_t[L - 1:L, :]
        b_last = b_col[L - 1:L, :]
        decay = jnp.exp(b_last + ms - m_new)
        g_col = jnp.exp(b_last - b_col + i_col - m_new)
        kg = k * g_col
        c_new = decay * cs + lax.dot_general(kg.astype(BF16), vb, tn,
                                             preferred_element_type=F32)
        n_new = decay * ns + jnp.sum(kg, axis=0, keepdims=True)
        hg = jax.nn.sigmoid(og_ref[pl.ds(r0, L), :]) * h
        mu = jnp.mean(hg, axis=-1, keepdims=True)
        hc = hg - mu
        var = jnp.mean(hc * hc, axis=-1, keepdims=True)
        y_ref[pl.ds(r0, L), j * HEAD_DIM:(j + 1) * HEAD_DIM] = (
            hc * lax.rsqrt(var + LN_EPS) * ng_ref[j]).astype(y_ref.dtype)
        return c_new, n_new, m_new

    def body(c, carry):
        r0 = pl.multiple_of(c * L, L)
        out = []
        for j in range(hb):
            out.extend(one_head(j, c, r0, *carry[3 * j:3 * j + 3]))
        return tuple(out)

    init = []
    for j in range(hb):
        init.extend((c0_ref[0, j], n0_ref[0, j], m0_ref[0, j]))
    final = lax.fori_loop(0, n_chunks, body, tuple(init))
    for j in range(hb):
        c_ref[0, j] = final[3 * j]
        n_ref[0, j] = final[3 * j + 1]
        m_ref[0, j] = final[3 * j + 2]


def _mlstm_group(proj, gates_t, b_i, b_f, norm_g, c0, n0, m0,
                 *, row0, batches, rows, n_valid, col0, name):
    heads = b_i.shape[0]
    hb = max(g for g in (4, 3, 2, 1) if heads % g == 0)
    chunk = MLSTM_CHUNK if n_valid % MLSTM_CHUNK == 0 else rows
    assert n_valid == rows or chunk == rows
    n_chunks = rows // chunk
    c0b = col0 // HEAD_DIM
    rb = row0 // rows
    g4 = gates_t[:, row0:row0 + batches * rows].reshape(2 * heads, batches, n_chunks, chunk)

    def tok(which, j):
        return pl.BlockSpec((rows, HEAD_DIM),
                            lambda b, h: (rb + b, c0b + which * heads + h * hb + j))

    gate = lambda off: pl.BlockSpec(
        (hb, 1, n_chunks, chunk), lambda b, h: (off // hb + h, b, 0, 0))
    scal = pl.BlockSpec((hb, 1, 1), lambda b, h: (h, 0, 0))
    cspec = pl.BlockSpec((1, hb, HEAD_DIM, HEAD_DIM), lambda b, h: (b, h, 0, 0))
    nspec = pl.BlockSpec((1, hb, 1, HEAD_DIM), lambda b, h: (b, h, 0, 0))
    mspec = pl.BlockSpec((1, hb, 1, 1), lambda b, h: (b, h, 0, 0))
    tok_specs = [tok(which, j) for j in range(hb) for which in range(4)]
    y, c, n, m = pl.pallas_call(
        functools.partial(_mlstm_kernel, chunk=chunk, n_chunks=n_chunks,
                          n_valid=min(n_valid, chunk), hb=hb),
        grid=(batches, heads // hb),
        in_specs=tok_specs + [gate(0), gate(heads), scal, scal,
                              pl.BlockSpec((hb, 1, HEAD_DIM), lambda b, h: (h, 0, 0)),
                              cspec, nspec, mspec],
        out_specs=[pl.BlockSpec((rows, hb * HEAD_DIM), lambda b, h: (b, h)), cspec, nspec, mspec],
        out_shape=[jax.ShapeDtypeStruct((batches * rows, heads * HEAD_DIM), BF16),
                   jax.ShapeDtypeStruct(c0.shape, F32),
                   jax.ShapeDtypeStruct(n0.shape[:2] + (1, HEAD_DIM), F32),
                   jax.ShapeDtypeStruct(m0.shape + (1, 1), F32)],
        compiler_params=_params("parallel", "parallel"),
        name=name,
    )(*([proj] * (4 * hb)), g4, g4,
      b_i.reshape(heads, 1, 1), b_f.reshape(heads, 1, 1), norm_g.reshape(heads, 1, HEAD_DIM),
      c0, n0.reshape(n0.shape[:2] + (1, HEAD_DIM)), m0.reshape(m0.shape + (1, 1)))
    return y, c, n.reshape(n0.shape), m.reshape(m0.shape)


def _topk_rows(vals, payload, k):
    rows = vals.shape[0]
    ridx = lax.broadcasted_iota(jnp.int32, vals.shape, 0).astype(F32)
    out_v, out_p = [], []
    for _ in range(k):
        m = jnp.max(vals, axis=0, keepdims=True)
        first = jnp.min(jnp.where(vals == m, ridx, float(rows)), axis=0, keepdims=True)
        hit = ridx == first
        out_v.append(m)
        out_p.append(jnp.sum(jnp.where(hit, payload, 0.0), axis=0, keepdims=True))
        vals = jnp.where(hit, -jnp.inf, vals)
    return jnp.concatenate(out_v, axis=0), jnp.concatenate(out_p, axis=0)


def _route_kernel(q_ref, sk_ref, g_ref, it_ref, jt_ref, gt_ref, w_ref, *, heads, n_keys):
    tok = q_ref.shape[0]
    half = sk_ref.shape[-1]
    K = PEER_TOPK
    nt = (((1,), (1,)), ((), ()))
    key_id = lax.broadcasted_iota(jnp.int32, (n_keys, tok), 0).astype(F32)
    i_rows, j_rows, g_rows = [], [], []
    for h in range(heads):
        sv, si = [], []
        for p in range(2):
            c = (h * 2 + p) * half
            qh = q_ref[:, c:c + half].astype(BF16)
            st = lax.dot_general(sk_ref[h, p], qh, nt, preferred_element_type=F32)
            v, i = _topk_rows(st, key_id, K)
            sv.append(v)
            si.append(i)
        cand = jnp.concatenate([sv[0][a:a + 1] + sv[1] for a in range(K)], axis=0)
        cid = jnp.concatenate([si[0][a:a + 1] * n_keys + si[1] for a in range(K)], axis=0)
        cv, eid = _topk_rows(cand, cid, K)
        e = jnp.exp(cv - cv[0:1])
        gate = e / jnp.sum(e, axis=0, keepdims=True)
        ei = jnp.floor(eid * (1.0 / n_keys))
        i_rows.append(ei)
        j_rows.append(eid - ei * n_keys)
        g_rows.append(gate)
    it_ref[...] = jnp.concatenate(i_rows, axis=0).T
    jt_ref[...] = jnp.concatenate(j_rows, axis=0).T
    gt_ref[...] = jnp.concatenate(g_rows, axis=0).T

    picks = heads * K
    sub = lax.broadcasted_iota(jnp.int32, (n_keys, picks), 0).astype(F32)

    def per_token(n, _):
        irow = it_ref[pl.ds(n, 1), :]
        jrow = jt_ref[pl.ds(n, 1), :]
        grow = gt_ref[pl.ds(n, 1), :]
        a_t = jnp.where(sub == irow, 1.0, 0.0).astype(BF16)
        b_t = jnp.where(sub == jrow, grow, 0.0).astype(BF16)
        r0 = pl.multiple_of(n * n_keys, n_keys)
        w_ref[pl.ds(r0, n_keys), :] = lax.dot_general(a_t, b_t, nt, preferred_element_type=F32)
        return 0

    lax.fori_loop(0, tok, per_token, 0, unroll=ROUTE_UNROLL)
    for i in range(n_keys):
        g_ref[:, i * n_keys:(i + 1) * n_keys] = w_ref[pl.ds(i, tok, stride=n_keys), :].astype(g_ref.dtype)


def _route(qp, subkeys_bf, *, name):
    n = qp.shape[0]
    heads, _, n_keys, half = subkeys_bf.shape
    tok = LANES
    picks = heads * PEER_TOPK
    return pl.pallas_call(
        functools.partial(_route_kernel, heads=heads, n_keys=n_keys),
        grid=(n // tok,),
        in_specs=[pl.BlockSpec((tok, qp.shape[1]), lambda i: (i, 0)),
                  pl.BlockSpec(subkeys_bf.shape, lambda i: (0, 0, 0, 0))],
        out_specs=pl.BlockSpec((tok, n_keys * n_keys), lambda i: (i, 0)),
        out_shape=jax.ShapeDtypeStruct((n, n_keys * n_keys), BF16),
        scratch_shapes=[pltpu.VMEM((tok, picks), F32)] * 3
        + [pltpu.VMEM((tok * n_keys, n_keys), F32)],
        compiler_params=_params("parallel"),
        name=name,
    )(qp, subkeys_bf)


def _peer_act_kernel(x_ref, u_ref, g_ref, p_ref):
    s = lax.dot_general(x_ref[...], u_ref[...], (((1,), (1,)), ((), ())),
                        preferred_element_type=F32)
    act = 0.5 * s * (1.0 + lax.erf(s * (1.0 / math.sqrt(2.0))))
    p_ref[...] = (act * g_ref[...].astype(F32)).astype(p_ref.dtype)


def _peer_act(xb, u, g, *, tn, te, name):
    n, d = xb.shape
    n_exp = u.shape[0]
    return pl.pallas_call(
        _peer_act_kernel,
        grid=(n // tn, n_exp // te),
        in_specs=[pl.BlockSpec((tn, d), lambda i, e: (i, 0)),
                  pl.BlockSpec((te, d), lambda i, e: (e, 0)),
                  pl.BlockSpec((tn, te), lambda i, e: (i, e))],
        out_specs=pl.BlockSpec((tn, te), lambda i, e: (i, e)),
        out_shape=jax.ShapeDtypeStruct((n, n_exp), BF16),
        compiler_params=_params("parallel", "parallel"),
        name=name,
    )(xb, u, g)


def _layer(x, xb, st, w, *, dims, alpha, tag):
    bp, tp, bs, ts = dims
    n, d = x.shape
    rows_p = bp * tp
    conv_ch = w["conv_w"].shape[-1]
    a_heads = st["cache_k"].shape[2]
    m_heads = w["b_i"].shape[0]
    attn_w = a_heads * HEAD_DIM
    mlstm_w = m_heads * HEAD_DIM
    off_att = 2 * conv_ch
    off_mls = off_att + 3 * attn_w
    off_gate = off_mls + 4 * mlstm_w

    tn_big = _pick_tile(n, 1664, SUBLANES_BF16)
    w_in = w["w_in"].astype(BF16)
    proj = _matmul(xb, w_in[:, :off_gate], tn=tn_big,
                   tm=_pick_tile(off_gate, 512, LANES), name=f"proj{tag}")
    w_gate = jnp.pad(w_in[:, off_gate:], ((0, 0), (0, LANES - 2 * m_heads)))
    gates = _matmul(xb, w_gate, tn=tn_big, tm=LANES, name=f"gates{tag}")
    gates_t = gates[:, :2 * m_heads].T

    zeros = lambda *s: jnp.zeros(s, F32)
    conv_args = (w["conv_w"], w["conv_b"], w["conv_ln_g"], w["conv_ln_b"])
    ya_p, conv_p = _conv_group(proj, zeros(bp, w["conv_w"].shape[0] - 1, conv_ch), *conv_args,
                               row0=0, batches=bp, rows=tp, n_valid=tp, name=f"conv_p{tag}")
    ya_s, conv_s = _conv_group(proj, st["conv"], *conv_args, row0=rows_p, batches=bs,
                               rows=SAMPLE_ROWS, n_valid=ts, name=f"conv_s{tag}")

    yb_p = _attn_prompt(proj, batches=bp, t_len=tp, heads=a_heads, col0=off_att,
                        name=f"attn_p{tag}")
    yb_s = _attn_sample(proj, st["cache_k"], st["cache_v"], row0=rows_p, batches=bs,
                        heads=a_heads, col0=off_att, n_valid=ts, name=f"attn_s{tag}")

    ml_args = (w["b_i"], w["b_f"], w["norm_g"])
    yc_p, c_p, n_p, m_p = _mlstm_group(
        proj, gates_t, *ml_args, zeros(bp, m_heads, HEAD_DIM, HEAD_DIM),
        zeros(bp, m_heads, HEAD_DIM), zeros(bp, m_heads),
        row0=0, batches=bp, rows=tp, n_valid=tp, col0=off_mls, name=f"mlstm_p{tag}")
    yc_s, c_s, n_s, m_s = _mlstm_group(
        proj, gates_t, *ml_args, st["c"], st["n"], st["m"],
        row0=rows_p, batches=bs, rows=SAMPLE_ROWS, n_valid=ts, col0=off_mls,
        name=f"mlstm_s{tag}")

    rows_s = bs * SAMPLE_ROWS
    ymix = jnp.concatenate([
        jnp.concatenate([ya_p, yb_p, yc_p], axis=1),
        jnp.concatenate([ya_s, yb_s, yc_s], axis=1),
        jnp.zeros((n - rows_p - rows_s, d), BF16)], axis=0)
    mix = _matmul(ymix, w["w_out"].astype(BF16), tn=tn_big,
                  tm=_pick_tile(d, 512, LANES), name=f"outproj{tag}")
    tn_ln = _pick_tile(n, 256, SUBLANES_BF16)
    x1, x1b = _res_ln(x, mix, w["ln1_g"], w["ln1_b"], alpha=alpha, tn=tn_ln, name=f"ln1{tag}")

    qp = _matmul(x1b, w["peer_wq"].astype(BF16), tn=tn_big,
                 tm=_pick_tile(w["peer_wq"].shape[1], 512, LANES), name=f"peer_q{tag}")
    g = _route(qp, w["peer_subkeys"].astype(BF16), name=f"route{tag}")
    n_exp = w["peer_u"].shape[0]
    p = _peer_act(x1b, w["peer_u"].astype(BF16), g, tn=tn_big,
                  te=_pick_tile(n_exp, 512, LANES), name=f"peer_act{tag}")
    po = _matmul_ktiled(p, w["peer_v"].astype(BF16), tn=tn_big, tm=_pick_tile(d, 1024, LANES),
                        tk=_pick_tile(n_exp, 2048, LANES), name=f"peer_out{tag}")
    x2, x2b = _res_ln(x1, po, w["ln2_g"], w["ln2_b"], alpha=alpha, tn=tn_ln, name=f"ln2{tag}")

    win = min(max(wd for wd, _ in DILATED_PATTERNS), tp)
    kv = lambda rows, b, t, which: rows[:, off_att + which * attn_w:off_att + (which + 1) * attn_w
                                        ].reshape(b, t, a_heads, HEAD_DIM)
    pr = proj[:rows_p]
    sr = proj[rows_p:rows_p + rows_s]
    outs = dict(
        k_p=kv(pr, bp, tp, 1)[:, tp - win:], v_p=kv(pr, bp, tp, 2)[:, tp - win:],
        k_s=kv(sr, bs, SAMPLE_ROWS, 1)[:, :ts], v_s=kv(sr, bs, SAMPLE_ROWS, 2)[:, :ts],
        conv_p=conv_p, conv_s=conv_s, c_p=c_p, c_s=c_s, n_p=n_p, n_s=n_s, m_p=m_p, m_s=m_s)
    return x2, x2b, outs


def kernel(x_prompt, x_sample, cache_attn_k, cache_attn_v, state_conv, state_mlstm_c, state_mlstm_n, state_mlstm_m, w_in, conv_w, conv_b, conv_ln_g, conv_ln_b, mlstm_b_i, mlstm_b_f, mlstm_norm_g, w_out, ln1_g, ln1_b, peer_wq, peer_subkeys, peer_u, peer_v, ln2_g, ln2_b):
    bp, tp, d = x_prompt.shape
    bs, ts, _ = x_sample.shape
    depth = w_in.shape[0]
    assert ts <= SAMPLE_ROWS and tp % SAMPLE_ROWS == 0
    rows_p = bp * tp
    rows_s = bs * SAMPLE_ROWS
    n = -(-(rows_p + rows_s) // LANES) * LANES
    xs = jnp.pad(x_sample, ((0, 0), (0, SAMPLE_ROWS - ts), (0, 0))).reshape(rows_s, d)
    x = jnp.concatenate([x_prompt.reshape(rows_p, d), xs,
                         jnp.zeros((n - rows_p - rows_s, d), x_prompt.dtype)], axis=0)
    xb = x.astype(BF16)
    alpha = (2 * depth) ** 0.25

    per_layer = []
    for l in range(depth):
        st = dict(cache_k=cache_attn_k[l], cache_v=cache_attn_v[l], conv=state_conv[l],
                  c=state_mlstm_c[l], n=state_mlstm_n[l], m=state_mlstm_m[l])
        w = dict(w_in=w_in[l], conv_w=conv_w[l], conv_b=conv_b[l], conv_ln_g=conv_ln_g[l],
                 conv_ln_b=conv_ln_b[l], b_i=mlstm_b_i[l], b_f=mlstm_b_f[l],
                 norm_g=mlstm_norm_g[l], w_out=w_out[l], ln1_g=ln1_g[l], ln1_b=ln1_b[l],
                 peer_wq=peer_wq[l], peer_subkeys=peer_subkeys[l], peer_u=peer_u[l],
                 peer_v=peer_v[l], ln2_g=ln2_g[l], ln2_b=ln2_b[l])
        x, xb, outs = _layer(x, xb, st, w, dims=(bp, tp, bs, ts), alpha=alpha, tag=str(l))
        per_layer.append(outs)

    stack = lambda key: jnp.stack([o[key] for o in per_layer])
    y_prompt = x[:rows_p].reshape(bp, tp, d)
    y_sample = x[rows_p:rows_p + rows_s].reshape(bs, SAMPLE_ROWS, d)[:, :ts]
    return (y_prompt, y_sample, stack("k_p"), stack("v_p"), stack("k_s"), stack("v_s"),
            stack("conv_p"), stack("conv_s"), stack("c_p"), stack("c_s"),
            stack("n_p"), stack("n_s"), stack("m_p"), stack("m_s"))
```

```python
import functools
import math

import jax
import jax.numpy as jnp
from jax import lax
from jax.experimental import pallas as pl
from jax.experimental.pallas import tpu as pltpu

F32 = jnp.float32
BF16 = jnp.bfloat16

LANES = 128
SUBLANES_BF16 = 16
VMEM_LIMIT_BYTES = 56 * 1024 * 1024

HEAD_DIM = 128
DILATED_PATTERNS = ((128, 1), (512, 4), (2048, 16))
MLSTM_CHUNK = 64
PEER_TOPK = 16
LN_EPS = 1e-5
SAMPLE_ROWS = 16
CONV_HALO = 32
MASKED = -1e30
ROUTE_UNROLL = 8


def _params(*sem):
    return pltpu.CompilerParams(dimension_semantics=sem,
                                vmem_limit_bytes=VMEM_LIMIT_BYTES)


def _pick_tile(n, target, align):
    best = None
    for t in range(align, min(n, target) + 1, align):
        if n % t == 0:
            best = t
    assert best is not None, (n, target, align)
    return best


def _mm_kernel(a_ref, w_ref, o_ref):
    o_ref[...] = jnp.dot(a_ref[...], w_ref[...],
                         preferred_element_type=F32).astype(o_ref.dtype)


def _matmul(a, w, *, layer, width, tn, tm, name):
    n, k = a.shape
    return pl.pallas_call(
        _mm_kernel,
        grid=(n // tn, width // tm),
        in_specs=[pl.BlockSpec((tn, k), lambda i, j: (i, 0)),
                  pl.BlockSpec((None, k, tm), lambda i, j: (layer, 0, j))],
        out_specs=pl.BlockSpec((tn, tm), lambda i, j: (i, j)),
        out_shape=jax.ShapeDtypeStruct((n, width), F32),
        compiler_params=_params("parallel", "parallel"),
        name=name,
    )(a, w)


def _mm_acc_kernel(a_ref, w_ref, o_ref):
    part = jnp.dot(a_ref[...], w_ref[...].astype(BF16), preferred_element_type=F32)

    @pl.when(pl.program_id(2) == 0)
    def _():
        o_ref[...] = part

    @pl.when(pl.program_id(2) > 0)
    def _():
        o_ref[...] += part


def _matmul_ktiled(a, w, *, layer, tn, tm, tk, name):
    n, k = a.shape
    m = w.shape[2]
    return pl.pallas_call(
        _mm_acc_kernel,
        grid=(n // tn, m // tm, k // tk),
        in_specs=[pl.BlockSpec((tn, tk), lambda i, j, q: (i, q)),
                  pl.BlockSpec((None, tk, tm), lambda i, j, q: (layer, q, j))],
        out_specs=pl.BlockSpec((tn, tm), lambda i, j, q: (i, j)),
        out_shape=jax.ShapeDtypeStruct((n, m), F32),
        compiler_params=_params("parallel", "parallel", "arbitrary"),
        name=name,
    )(a, w)


def _res_ln_kernel(x_ref, y_ref, g_ref, b_ref, o_ref, ob_ref, *, alpha):
    z = alpha * x_ref[...] + y_ref[...]
    mu = jnp.mean(z, axis=-1, keepdims=True)
    zc = z - mu
    var = jnp.mean(zc * zc, axis=-1, keepdims=True)
    out = zc * lax.rsqrt(var + LN_EPS) * g_ref[...] + b_ref[...]
    o_ref[...] = out
    ob_ref[...] = out.astype(BF16)


def _res_ln(x, y, g, b, *, alpha, tn, name):
    n, d = x.shape
    row = pl.BlockSpec((tn, d), lambda i: (i, 0))
    vec = pl.BlockSpec((1, d), lambda i: (0, 0))
    return pl.pallas_call(
        functools.partial(_res_ln_kernel, alpha=alpha),
        grid=(n // tn,),
        in_specs=[row, row, vec, vec],
        out_specs=[row, row],
        out_shape=[jax.ShapeDtypeStruct((n, d), F32),
                   jax.ShapeDtypeStruct((n, d), BF16)],
        compiler_params=_params("parallel"),
        name=name,
    )(x, y, g.reshape(1, d), b.reshape(1, d))


def _conv_kernel(a_ref, gate_ref, prev_ref, cw_ref, cb_ref, lg_ref, lb_ref,
                 y_ref, new_ref, ext_ref, *, tt, n_valid_last, width):
    t = pl.program_id(1)
    keep = width - 1

    @pl.when(t == 0)
    def _():
        ext_ref[0:CONV_HALO, :] = jnp.zeros((CONV_HALO, ext_ref.shape[1]), F32)
        ext_ref[CONV_HALO - keep:CONV_HALO, :] = prev_ref[0]

    ext_ref[CONV_HALO:CONV_HALO + tt, :] = a_ref[...] * jax.nn.sigmoid(gate_ref[...])
    acc = jnp.zeros((tt, ext_ref.shape[1]), F32)
    for w in range(width):
        lo = CONV_HALO - keep + w
        acc = acc + ext_ref[lo:lo + tt, :] * cw_ref[w:w + 1, :]
    y = acc + cb_ref[...]
    mu = jnp.mean(y, axis=-1, keepdims=True)
    yc = y - mu
    var = jnp.mean(yc * yc, axis=-1, keepdims=True)
    y = yc * lax.rsqrt(var + LN_EPS) * lg_ref[...] + lb_ref[...]
    y_ref[...] = (y * jax.nn.sigmoid(y)).astype(y_ref.dtype)

    @pl.when(t == pl.num_programs(1) - 1)
    def _():
        hi = CONV_HALO + n_valid_last
        new_ref[0] = ext_ref[hi - keep:hi, :]

    ext_ref[0:CONV_HALO, :] = ext_ref[tt:tt + CONV_HALO, :]


def _conv_group(proj, prev, cw, cb, lg, lb, *, row0, batches, rows, n_valid, name):
    width, ch = cw.shape
    tt = _pick_tile(rows, 256, 8)
    nt = rows // tt
    blk0 = row0 // tt
    assert row0 % tt == 0 and n_valid > rows - tt
    n_valid_last = n_valid - (nt - 1) * tt
    rowspec = lambda col: pl.BlockSpec((tt, ch), lambda b, t: (blk0 + b * nt + t, col))
    vec = pl.BlockSpec((1, ch), lambda b, t: (0, 0))
    state = pl.BlockSpec((1, width - 1, ch), lambda b, t: (b, 0, 0))
    return pl.pallas_call(
        functools.partial(_conv_kernel, tt=tt, n_valid_last=n_valid_last, width=width),
        grid=(batches, nt),
        in_specs=[rowspec(0), rowspec(1), state,
                  pl.BlockSpec((width, ch), lambda b, t: (0, 0)), vec, vec, vec],
        out_specs=[pl.BlockSpec((tt, ch), lambda b, t: (b * nt + t, 0)), state],
        out_shape=[jax.ShapeDtypeStruct((batches * rows, ch), BF16),
                   jax.ShapeDtypeStruct((batches, width - 1, ch), F32)],
        scratch_shapes=[pltpu.VMEM((CONV_HALO + tt + CONV_HALO, ch), F32)],
        compiler_params=_params("parallel", "arbitrary"),
        name=name,
    )(proj, proj, prev, cw, cb.reshape(1, ch), lg.reshape(1, ch), lb.reshape(1, ch))


def _pattern_count(delta):
    cnt = jnp.zeros(delta.shape, F32)
    for window, dil in DILATED_PATTERNS:
        hit = (delta <= window) & ((delta & (dil - 1)) == 0)
        cnt = cnt + jnp.where(hit, 1.0, 0.0)
    return jnp.where(delta >= 0, cnt, 0.0)


def _attn_prompt_kernel(q_ref, k_ref, v_ref, o_ref, *, qb, scale):
    t_len = q_ref.shape[0]
    kb = k_ref[...].astype(BF16)
    vb = v_ref[...].astype(BF16)
    for i in range(t_len // qb):
        hi = (i + 1) * qb
        q = q_ref[i * qb:hi, :].astype(BF16)
        s = lax.dot_general(q, kb[0:hi], (((1,), (1,)), ((), ())),
                            preferred_element_type=F32) * scale
        qpos = i * qb + lax.broadcasted_iota(jnp.int32, (qb, hi), 0)
        kpos = lax.broadcasted_iota(jnp.int32, (qb, hi), 1)
        cnt = _pattern_count(qpos - kpos)
        s = jnp.where(cnt > 0, s, MASKED)
        mx = jnp.max(s, axis=-1, keepdims=True)
        e = jnp.exp(s - mx) * cnt
        den = jnp.sum(e, axis=-1, keepdims=True)
        out = jnp.dot(e.astype(BF16), vb[0:hi], preferred_element_type=F32)
        o_ref[i * qb:hi, :] = (out / den).astype(o_ref.dtype)


def _attn_prompt(proj, *, batches, t_len, heads, col0, name):
    c0 = col0 // HEAD_DIM
    qb = _pick_tile(t_len, 256, 8)
    spec = lambda which: pl.BlockSpec(
        (t_len, HEAD_DIM), lambda b, h: (b, c0 + which * heads + h))
    return pl.pallas_call(
        functools.partial(_attn_prompt_kernel, qb=qb, scale=HEAD_DIM ** -0.5),
        grid=(batches, heads),
        in_specs=[spec(0), spec(1), spec(2)],
        out_specs=pl.BlockSpec((t_len, HEAD_DIM), lambda b, h: (b, h)),
        out_shape=jax.ShapeDtypeStruct((batches * t_len, heads * HEAD_DIM), BF16),
        compiler_params=_params("parallel", "parallel"),
        name=name,
    )(proj, proj, proj)


def _attn_sample_kernel(q_ref, k_ref, v_ref, kc_ref, vc_ref, o_ref, *, n_valid, scale):
    rows = q_ref.shape[0]
    n_prev = kc_ref.shape[1]
    q = q_ref[...].astype(BF16)
    nt = (((1,), (1,)), ((), ()))
    s_c = lax.dot_general(q, kc_ref[0].astype(BF16), nt, preferred_element_type=F32) * scale
    s_n = lax.dot_general(q, k_ref[...].astype(BF16), nt, preferred_element_type=F32) * scale
    qpos_c = n_prev + lax.broadcasted_iota(jnp.int32, (rows, n_prev), 0)
    cnt_c = _pattern_count(qpos_c - lax.broadcasted_iota(jnp.int32, (rows, n_prev), 1))
    tq = lax.broadcasted_iota(jnp.int32, (rows, rows), 0)
    tk = lax.broadcasted_iota(jnp.int32, (rows, rows), 1)
    cnt_n = jnp.where(tk < n_valid, _pattern_count(tq - tk), 0.0)
    s_c = jnp.where(cnt_c > 0, s_c, MASKED)
    s_n = jnp.where(cnt_n > 0, s_n, MASKED)
    mx = jnp.maximum(jnp.max(s_c, axis=-1, keepdims=True),
                     jnp.max(s_n, axis=-1, keepdims=True))
    e_c = jnp.exp(s_c - mx) * cnt_c
    e_n = jnp.exp(s_n - mx) * cnt_n
    den = jnp.sum(e_c, axis=-1, keepdims=True) + jnp.sum(e_n, axis=-1, keepdims=True)
    out = (jnp.dot(e_c.astype(BF16), vc_ref[0].astype(BF16), preferred_element_type=F32)
           + jnp.dot(e_n.astype(BF16), v_ref[...].astype(BF16), preferred_element_type=F32))
    o_ref[...] = (out / den).astype(o_ref.dtype)


def _attn_sample(proj, cache_k, cache_v, *, layer, row0, batches, heads, col0, n_valid, name):
    c0 = col0 // HEAD_DIM
    r0 = row0 // SAMPLE_ROWS
    n_prev = cache_k.shape[2]
    new = lambda which: pl.BlockSpec(
        (SAMPLE_ROWS, HEAD_DIM), lambda b, h: (r0 + b, c0 + which * heads + h))
    cache = pl.BlockSpec((None, 1, n_prev, HEAD_DIM), lambda b, h: (layer, b, 0, h))
    return pl.pallas_call(
        functools.partial(_attn_sample_kernel, n_valid=n_valid, scale=HEAD_DIM ** -0.5),
        grid=(batches, heads),
        in_specs=[new(0), new(1), new(2), cache, cache],
        out_specs=pl.BlockSpec((SAMPLE_ROWS, HEAD_DIM), lambda b, h: (b, h)),
        out_shape=jax.ShapeDtypeStruct((batches * SAMPLE_ROWS, heads * HEAD_DIM), BF16),
        compiler_params=_params("parallel", "parallel"),
        name=name,
    )(proj, proj, proj, cache_k, cache_v)


def _mlstm_kernel(*refs, chunk, n_chunks, n_valid, hb):
    tok_refs = refs[:4 * hb]
    (gi_ref, gf_ref, bi_ref, bf_ref, ng_ref, c0_ref, n0_ref, m0_ref,
     y_ref, c_ref, n_ref, m_ref) = refs[4 * hb:]
    L = chunk
    row = lax.broadcasted_iota(jnp.int32, (L, L), 0)
    col = lax.broadcasted_iota(jnp.int32, (L, L), 1)
    causal = col <= row
    eye = jnp.where(row == col, 1.0, 0.0)
    upper = jnp.where(row <= col, 1.0, 0.0)
    lower = jnp.where(causal, 1.0, 0.0)
    lane = lax.broadcasted_iota(jnp.int32, (1, L), 1)
    tn = (((0,), (0,)), ((), ()))
    nt = (((1,), (1,)), ((), ()))

    def to_col(r):
        return jnp.sum(eye * r, axis=-1, keepdims=True)

    def one_head(j, c, r0, cs, ns, ms):
        q_ref, k_ref, v_ref, og_ref = tok_refs[4 * j:4 * j + 4]
        q = q_ref[pl.ds(r0, L), :]
        k = k_ref[pl.ds(r0, L), :] * (HEAD_DIM ** -0.5)
        v = v_ref[pl.ds(r0, L), :]
        i_row = gi_ref[j, 0, pl.ds(c, 1), :] + bi_ref[j]
        f_pre = gf_ref[j, 0, pl.ds(c, 1), :] + bf_ref[j]
        f_row = jnp.minimum(f_pre, 0.0) - jnp.log(1.0 + jnp.exp(-jnp.abs(f_pre)))
        if n_valid < L:
            i_row = jnp.where(lane < n_valid, i_row, MASKED)
            f_row = jnp.where(lane < n_valid, f_row, 0.0)
        i_col = to_col(i_row)
        f_col = to_col(f_row)
        b_col = jnp.sum(lower * f_row, axis=-1, keepdims=True)
        b_row = jnp.sum(upper * f_col, axis=0, keepdims=True)
        a_intra = jnp.where(causal, b_col - b_row + i_row, MASKED)
        a_inter = b_col + ms
        m_t = jnp.maximum(a_inter, jnp.max(a_intra, axis=-1, keepdims=True))
        w_intra = jnp.exp(a_intra - m_t)
        w_inter = jnp.exp(a_inter - m_t)
        qb = q.astype(BF16)
        kb = k.astype(BF16)
        vb = v.astype(BF16)
        qk = lax.dot_general(qb, kb, nt, preferred_element_type=F32) * w_intra
        num = (w_inter * jnp.dot(qb, cs.astype(BF16), preferred_element_type=F32)
               + jnp.dot(qk.astype(BF16), vb, preferred_element_type=F32))
        qn = (w_inter * jnp.sum(q * ns, axis=-1, keepdims=True)
              + jnp.sum(qk, axis=-1, keepdims=True))
        h = num / jnp.maximum(jnp.abs(qn), jnp.exp(-m_t))
        m_new = m_t[L - 1:L, :]
        b_last = b_col[L - 1:L, :]
        decay = jnp.exp(b_last + ms - m_new)
        g_col = jnp.exp(b_last - b_col + i_col - m_new)
        kg = k * g_col
        c_new = decay * cs + lax.dot_general(kg.astype(BF16), vb, tn,
                                             preferred_element_type=F32)
        n_new = decay * ns + jnp.sum(kg, axis=0, keepdims=True)
        hg = jax.nn.sigmoid(og_ref[pl.ds(r0, L), :]) * h
        mu = jnp.mean(hg, axis=-1, keepdims=True)
        hc = hg - mu
        var = jnp.mean(hc * hc, axis=-1, keepdims=True)
        y_ref[pl.ds(r0, L), j * HEAD_DIM:(j + 1) * HEAD_DIM] = (
            hc * lax.rsqrt(var + LN_EPS) * ng_ref[j]).astype(y_ref.dtype)
        return c_new, n_new, m_new

    def body(c, carry):
        r0 = pl.multiple_of(c * L, L)
        out = []
        for j in range(hb):
            out.extend(one_head(j, c, r0, *carry[3 * j:3 * j + 3]))
        return tuple(out)

    init = []
    for j in range(hb):
        init.extend((c0_ref[0, j], n0_ref[0, j], m0_ref[0, j]))
    final = lax.fori_loop(0, n_chunks, body, tuple(init))
    for j in range(hb):
        c_ref[0, j] = final[3 * j]
        n_ref[0, j] = final[3 * j + 1]
        m_ref[0, j] = final[3 * j + 2]


def _mlstm_group(proj, gates_t, b_i, b_f, norm_g, c0, n0, m0,
                 *, row0, batches, rows, n_valid, col0, name):
    heads = b_i.shape[0]
    hb = max(g for g in (4, 3, 2, 1) if heads % g == 0)
    chunk = MLSTM_CHUNK if n_valid % MLSTM_CHUNK == 0 else rows
    assert n_valid == rows or chunk == rows
    n_chunks = rows // chunk
    c0b = col0 // HEAD_DIM
    rb = row0 // rows
    g4 = gates_t[:, row0:row0 + batches * rows].reshape(2 * heads, batches, n_chunks, chunk)

    def tok(which, j):
        return pl.BlockSpec((rows, HEAD_DIM),
                            lambda b, h: (rb + b, c0b + which * heads + h * hb + j))

    gate = lambda off: pl.BlockSpec(
        (hb, 1, n_chunks, chunk), lambda b, h: (off // hb + h, b, 0, 0))
    scal = pl.BlockSpec((hb, 1, 1), lambda b, h: (h, 0, 0))
    cspec = pl.BlockSpec((1, hb, HEAD_DIM, HEAD_DIM), lambda b, h: (b, h, 0, 0))
    nspec = pl.BlockSpec((1, hb, 1, HEAD_DIM), lambda b, h: (b, h, 0, 0))
    mspec = pl.BlockSpec((1, hb, 1, 1), lambda b, h: (b, h, 0, 0))
    tok_specs = [tok(which, j) for j in range(hb) for which in range(4)]
    y, c, n, m = pl.pallas_call(
        functools.partial(_mlstm_kernel, chunk=chunk, n_chunks=n_chunks,
                          n_valid=min(n_valid, chunk), hb=hb),
        grid=(batches, heads // hb),
        in_specs=tok_specs + [gate(0), gate(heads), scal, scal,
                              pl.BlockSpec((hb, 1, HEAD_DIM), lambda b, h: (h, 0, 0)),
                              cspec, nspec, mspec],
        out_specs=[pl.BlockSpec((rows, hb * HEAD_DIM), lambda b, h: (b, h)), cspec, nspec, mspec],
        out_shape=[jax.ShapeDtypeStruct((batches * rows, heads * HEAD_DIM), BF16),
                   jax.ShapeDtypeStruct(c0.shape, F32),
                   jax.ShapeDtypeStruct(n0.shape[:2] + (1, HEAD_DIM), F32),
                   jax.ShapeDtypeStruct(m0.shape + (1, 1), F32)],
        compiler_params=_params("parallel", "parallel"),
        name=name,
    )(*([proj] * (4 * hb)), g4, g4,
      b_i.reshape(heads, 1, 1), b_f.reshape(heads, 1, 1), norm_g.reshape(heads, 1, HEAD_DIM),
      c0, n0.reshape(n0.shape[:2] + (1, HEAD_DIM)), m0.reshape(m0.shape + (1, 1)))
    return y, c, n.reshape(n0.shape), m.reshape(m0.shape)


def _topk_rows(vals, order, k, payload=None):
    out_v, out_p = [], []
    for _ in range(k):
        m = jnp.max(vals, axis=0, keepdims=True)
        first = jnp.min(jnp.where(vals == m, order, float(2 ** 20)), axis=0, keepdims=True)
        hit = order == first
        out_v.append(m)
        if payload is None:
            out_p.append(first)
        else:
            out_p.append(jnp.sum(jnp.where(hit, payload, 0.0), axis=0, keepdims=True))
        vals = jnp.where(hit, -jnp.inf, vals)
    return jnp.concatenate(out_v, axis=0), jnp.concatenate(out_p, axis=0)


def _candidate_pieces(k):
    split = 5
    valid = lambda a, b: (a + 1) * (b + 1) <= k
    pieces = []
    for b in range(min(split, k)):
        for a0 in range(0, k, 8):
            if any(valid(a, b) for a in range(a0, a0 + 8)):
                pieces.append(("col", a0, b))
    for a in range(k):
        for b0 in range(0, k, 8):
            if any(valid(a, b) and b >= split for b in range(b0, b0 + 8)):
                pieces.append(("row", a, b0))
    covered = []
    for kind, p, q in pieces:
        for r in range(8):
            a, b = (p + r, q) if kind == "col" else (p, q + r)
            if valid(a, b) and (kind == "col" or b >= split):
                covered.append((a, b))
    assert sorted(covered) == sorted((a, b) for a in range(k) for b in range(k) if valid(a, b))
    return pieces, split


def _pair_candidates(sv, si, n_keys, k):
    tok = sv[0].shape[1]
    pieces, split = _candidate_pieces(k)
    r8 = lax.broadcasted_iota(jnp.int32, (8, tok), 0)
    vals, order, eid = [], [], []
    for piece, (kind, p, q) in enumerate(pieces):
        if kind == "col":
            a_idx, b_idx = p + r8, jnp.full((8, tok), q, jnp.int32)
            v = sv[0][p:p + 8] + sv[1][q:q + 1]
            e = si[0][p:p + 8] * n_keys + si[1][q:q + 1]
            ok = (a_idx + 1) * (q + 1) <= k
        else:
            a_idx, b_idx = jnp.full((8, tok), p, jnp.int32), q + r8
            v = sv[0][p:p + 1] + sv[1][q:q + 8]
            e = si[0][p:p + 1] * n_keys + si[1][q:q + 8]
            ok = ((p + 1) * (b_idx + 1) <= k) & (b_idx >= split)
        vals.append(jnp.where(ok, v, -jnp.inf))
        unused = k * k + piece * 8 + r8
        order.append(jnp.where(ok, a_idx * k + b_idx, unused).astype(F32))
        eid.append(e)
    return (jnp.concatenate(vals, axis=0), jnp.concatenate(order, axis=0),
            jnp.concatenate(eid, axis=0))


def _route_kernel(q_ref, sk_ref, g_ref, it_ref, jt_ref, gt_ref, w_ref, *, heads, n_keys):
    tok = q_ref.shape[0]
    half = sk_ref.shape[-1]
    K = PEER_TOPK
    nt = (((1,), (1,)), ((), ()))
    key_id = lax.broadcasted_iota(jnp.int32, (n_keys, tok), 0).astype(F32)
    i_rows, j_rows, g_rows = [], [], []
    for h in range(heads):
        sv, si = [], []
        for p in range(2):
            c = (h * 2 + p) * half
            qh = q_ref[:, c:c + half].astype(BF16)
            st = lax.dot_general(sk_ref[h, p], qh, nt, preferred_element_type=F32)
            v, i = _topk_rows(st, key_id, K)
            sv.append(v)
            si.append(i)
        cand, flat, cid = _pair_candidates(sv, si, n_keys, K)
        cv, eid = _topk_rows(cand, flat, K, payload=cid)
        e = jnp.exp(cv - cv[0:1])
        gate = e / jnp.sum(e, axis=0, keepdims=True)
        ei = jnp.floor(eid * (1.0 / n_keys))
        i_rows.append(ei)
        j_rows.append(eid - ei * n_keys)
        g_rows.append(gate)
    it_ref[...] = jnp.concatenate(i_rows, axis=0).T
    jt_ref[...] = jnp.concatenate(j_rows, axis=0).T
    gt_ref[...] = jnp.concatenate(g_rows, axis=0).T

    picks = heads * K
    sub = lax.broadcasted_iota(jnp.int32, (n_keys, picks), 0).astype(F32)

    def per_token(n, _):
        irow = it_ref[pl.ds(n, 1), :]
        jrow = jt_ref[pl.ds(n, 1), :]
        grow = gt_ref[pl.ds(n, 1), :]
        a_t = jnp.where(sub == irow, 1.0, 0.0).astype(BF16)
        b_t = jnp.where(sub == jrow, grow, 0.0).astype(BF16)
        r0 = pl.multiple_of(n * n_keys, n_keys)
        w_ref[pl.ds(r0, n_keys), :] = lax.dot_general(a_t, b_t, nt, preferred_element_type=F32)
        return 0

    lax.fori_loop(0, tok, per_token, 0, unroll=ROUTE_UNROLL)
    for i in range(n_keys):
        g_ref[:, i * n_keys:(i + 1) * n_keys] = w_ref[pl.ds(i, tok, stride=n_keys), :].astype(g_ref.dtype)


def _route(qp, subkeys_bf, *, name):
    n = qp.shape[0]
    heads, _, n_keys, half = subkeys_bf.shape
    tok = LANES
    picks = heads * PEER_TOPK
    return pl.pallas_call(
        functools.partial(_route_kernel, heads=heads, n_keys=n_keys),
        grid=(n // tok,),
        in_specs=[pl.BlockSpec((tok, qp.shape[1]), lambda i: (i, 0)),
                  pl.BlockSpec(subkeys_bf.shape, lambda i: (0, 0, 0, 0))],
        out_specs=pl.BlockSpec((tok, n_keys * n_keys), lambda i: (i, 0)),
        out_shape=jax.ShapeDtypeStruct((n, n_keys * n_keys), BF16),
        scratch_shapes=[pltpu.VMEM((tok, picks), F32)] * 3
        + [pltpu.VMEM((tok * n_keys, n_keys), F32)],
        compiler_params=_params("parallel"),
        name=name,
    )(qp, subkeys_bf)


def _peer_act_kernel(x_ref, u_ref, g_ref, p_ref):
    s = lax.dot_general(x_ref[...], u_ref[...].astype(BF16), (((1,), (1,)), ((), ())),
                        preferred_element_type=F32)
    act = 0.5 * s * (1.0 + lax.erf(s * (1.0 / math.sqrt(2.0))))
    p_ref[...] = (act * g_ref[...].astype(F32)).astype(p_ref.dtype)


def _peer_act(xb, u, g, *, layer, tn, te, name):
    n, d = xb.shape
    n_exp = u.shape[1]
    return pl.pallas_call(
        _peer_act_kernel,
        grid=(n // tn, n_exp // te),
        in_specs=[pl.BlockSpec((tn, d), lambda i, e: (i, 0), pipeline_mode=pl.Buffered(1)),
                  pl.BlockSpec((None, te, d), lambda i, e: (layer, e, 0)),
                  pl.BlockSpec((tn, te), lambda i, e: (i, e))],
        out_specs=pl.BlockSpec((tn, te), lambda i, e: (i, e)),
        out_shape=jax.ShapeDtypeStruct((n, n_exp), BF16),
        compiler_params=_params("parallel", "parallel"),
        name=name,
    )(xb, u, g)


def _layer(x, xb, st, w, big, *, layer, dims, alpha):
    bp, tp, bs, ts = dims
    tag = str(layer)
    n, d = x.shape
    rows_p = bp * tp
    conv_ch = w["conv_w"].shape[-1]
    a_heads = big["cache_k"].shape[3] // HEAD_DIM
    m_heads = w["b_i"].shape[0]
    attn_w = a_heads * HEAD_DIM
    mlstm_w = m_heads * HEAD_DIM
    off_att = 2 * conv_ch
    off_mls = off_att + 3 * attn_w
    off_gate = off_mls + 4 * mlstm_w

    tn_big = _pick_tile(n, 1664, SUBLANES_BF16)
    proj = _matmul(xb, big["w_in"], layer=layer, width=off_gate, tn=tn_big,
                   tm=_pick_tile(off_gate, 512, LANES), name=f"proj{tag}")
    gates = _matmul(xb, big["w_gate"], layer=layer, width=LANES, tn=tn_big, tm=LANES,
                    name=f"gates{tag}")
    gates_t = gates[:, :2 * m_heads].T

    zeros = lambda *s: jnp.zeros(s, F32)
    conv_args = (w["conv_w"], w["conv_b"], w["conv_ln_g"], w["conv_ln_b"])
    ya_p, conv_p = _conv_group(proj, zeros(bp, w["conv_w"].shape[0] - 1, conv_ch), *conv_args,
                               row0=0, batches=bp, rows=tp, n_valid=tp, name=f"conv_p{tag}")
    ya_s, conv_s = _conv_group(proj, st["conv"], *conv_args, row0=rows_p, batches=bs,
                               rows=SAMPLE_ROWS, n_valid=ts, name=f"conv_s{tag}")

    yb_p = _attn_prompt(proj, batches=bp, t_len=tp, heads=a_heads, col0=off_att,
                        name=f"attn_p{tag}")
    yb_s = _attn_sample(proj, big["cache_k"], big["cache_v"], layer=layer, row0=rows_p,
                        batches=bs, heads=a_heads, col0=off_att, n_valid=ts, name=f"attn_s{tag}")

    ml_args = (w["b_i"], w["b_f"], w["norm_g"])
    yc_p, c_p, n_p, m_p = _mlstm_group(
        proj, gates_t, *ml_args, zeros(bp, m_heads, HEAD_DIM, HEAD_DIM),
        zeros(bp, m_heads, HEAD_DIM), zeros(bp, m_heads),
        row0=0, batches=bp, rows=tp, n_valid=tp, col0=off_mls, name=f"mlstm_p{tag}")
    yc_s, c_s, n_s, m_s = _mlstm_group(
        proj, gates_t, *ml_args, st["c"], st["n"], st["m"],
        row0=rows_p, batches=bs, rows=SAMPLE_ROWS, n_valid=ts, col0=off_mls,
        name=f"mlstm_s{tag}")

    rows_s = bs * SAMPLE_ROWS
    ymix = jnp.concatenate([
        jnp.concatenate([ya_p, yb_p, yc_p], axis=1),
        jnp.concatenate([ya_s, yb_s, yc_s], axis=1),
        jnp.zeros((n - rows_p - rows_s, d), BF16)], axis=0)
    mix = _matmul(ymix, big["w_out"], layer=layer, width=d, tn=tn_big,
                  tm=_pick_tile(d, 512, LANES), name=f"outproj{tag}")
    tn_ln = _pick_tile(n, 256, SUBLANES_BF16)
    x1, x1b = _res_ln(x, mix, w["ln1_g"], w["ln1_b"], alpha=alpha, tn=tn_ln, name=f"ln1{tag}")

    q_dim = big["peer_wq"].shape[2]
    qp = _matmul(x1b, big["peer_wq"], layer=layer, width=q_dim, tn=tn_big,
                 tm=_pick_tile(q_dim, 512, LANES), name=f"peer_q{tag}")
    g = _route(qp, w["peer_subkeys"], name=f"route{tag}")
    n_exp = big["peer_u"].shape[1]
    p = _peer_act(x1b, big["peer_u"], g, layer=layer, tn=tn_big,
                  te=_pick_tile(n_exp, 512, LANES), name=f"peer_act{tag}")
    po = _matmul_ktiled(p, big["peer_v"], layer=layer, tn=tn_big, tm=_pick_tile(d, 1024, LANES),
                        tk=_pick_tile(n_exp, 2048, LANES), name=f"peer_out{tag}")
    x2, x2b = _res_ln(x1, po, w["ln2_g"], w["ln2_b"], alpha=alpha, tn=tn_ln, name=f"ln2{tag}")

    win = min(max(wd for wd, _ in DILATED_PATTERNS), tp)
    kv = lambda rows, b, t, which: rows[:, off_att + which * attn_w:off_att + (which + 1) * attn_w
                                        ].reshape(b, t, a_heads, HEAD_DIM)
    pr = proj[:rows_p]
    sr = proj[rows_p:rows_p + rows_s]
    outs = dict(
        k_p=kv(pr, bp, tp, 1)[:, tp - win:], v_p=kv(pr, bp, tp, 2)[:, tp - win:],
        k_s=kv(sr, bs, SAMPLE_ROWS, 1)[:, :ts], v_s=kv(sr, bs, SAMPLE_ROWS, 2)[:, :ts],
        conv_p=conv_p, conv_s=conv_s, c_p=c_p, c_s=c_s, n_p=n_p, n_s=n_s, m_p=m_p, m_s=m_s)
    return x2, x2b, outs


def kernel(x_prompt, x_sample, cache_attn_k, cache_attn_v, state_conv, state_mlstm_c, state_mlstm_n, state_mlstm_m, w_in, conv_w, conv_b, conv_ln_g, conv_ln_b, mlstm_b_i, mlstm_b_f, mlstm_norm_g, w_out, ln1_g, ln1_b, peer_wq, peer_subkeys, peer_u, peer_v, ln2_g, ln2_b):
    bp, tp, d = x_prompt.shape
    bs, ts, _ = x_sample.shape
    depth = w_in.shape[0]
    assert ts <= SAMPLE_ROWS and tp % SAMPLE_ROWS == 0
    rows_p = bp * tp
    rows_s = bs * SAMPLE_ROWS
    n = -(-(rows_p + rows_s) // LANES) * LANES
    xs = jnp.pad(x_sample, ((0, 0), (0, SAMPLE_ROWS - ts), (0, 0))).reshape(rows_s, d)
    x = jnp.concatenate([x_prompt.reshape(rows_p, d), xs,
                         jnp.zeros((n - rows_p - rows_s, d), x_prompt.dtype)], axis=0)
    xb = x.astype(BF16)
    alpha = (2 * depth) ** 0.25

    n_gate = 2 * mlstm_b_i.shape[1]
    w_in_b = w_in.astype(BF16)
    n_prev = cache_attn_k.shape[2]
    big = dict(
        w_in=w_in_b,
        w_gate=jnp.pad(w_in_b[:, :, w_in.shape[2] - n_gate:], ((0, 0), (0, 0), (0, LANES - n_gate))),
        w_out=w_out.astype(BF16), peer_wq=peer_wq.astype(BF16), peer_u=peer_u, peer_v=peer_v,
        cache_k=cache_attn_k.reshape(depth, bs, n_prev, -1),
        cache_v=cache_attn_v.reshape(depth, bs, n_prev, -1))
    subkeys_b = peer_subkeys.astype(BF16)

    per_layer = []
    for l in range(depth):
        st = dict(conv=state_conv[l], c=state_mlstm_c[l], n=state_mlstm_n[l], m=state_mlstm_m[l])
        w = dict(conv_w=conv_w[l], conv_b=conv_b[l], conv_ln_g=conv_ln_g[l],
                 conv_ln_b=conv_ln_b[l], b_i=mlstm_b_i[l], b_f=mlstm_b_f[l],
                 norm_g=mlstm_norm_g[l], ln1_g=ln1_g[l], ln1_b=ln1_b[l],
                 peer_subkeys=subkeys_b[l], ln2_g=ln2_g[l], ln2_b=ln2_b[l])
        x, xb, outs = _layer(x, xb, st, w, big, layer=l, dims=(bp, tp, bs, ts), alpha=alpha)
        per_layer.append(outs)

    stack = lambda key: jnp.stack([o[key] for o in per_layer])
    y_prompt = x[:rows_p].reshape(bp, tp, d)
    y_sample = x[rows_p:rows_p + rows_s].reshape(bs, SAMPLE_ROWS, d)[:, :ts]
    return (y_prompt, y_sample, stack("k_p"), stack("v_p"), stack("k_s"), stack("v_s"),
            stack("conv_p"), stack("conv_s"), stack("c_p"), stack("c_s"),
            stack("n_p"), stack("n_s"), stack("m_p"), stack("m_s"))
```

```python
import functools
import math

import jax
import jax.numpy as jnp
from jax import lax
from jax.experimental import pallas as pl
from jax.experimental.pallas import tpu as pltpu

F32 = jnp.float32
BF16 = jnp.bfloat16

LANES = 128
SUBLANES_BF16 = 16
VMEM_LIMIT_BYTES = 56 * 1024 * 1024

HEAD_DIM = 128
DILATED_PATTERNS = ((128, 1), (512, 4), (2048, 16))
MLSTM_CHUNK = 64
PEER_TOPK = 16
LN_EPS = 1e-5
SAMPLE_ROWS = 16
CONV_HALO = 32
MASKED = -1e30
ROUTE_UNROLL = 8


def _params(*sem):
    return pltpu.CompilerParams(dimension_semantics=sem,
                                vmem_limit_bytes=VMEM_LIMIT_BYTES)


def _pick_tile(n, target, align):
    best = None
    for t in range(align, min(n, target) + 1, align):
        if n % t == 0:
            best = t
    assert best is not None, (n, target, align)
    return best


def _mm_kernel(a_ref, w_ref, o_ref):
    o_ref[...] = jnp.dot(a_ref[...], w_ref[...],
                         preferred_element_type=F32).astype(o_ref.dtype)


def _matmul(a, w, *, layer, width, tn, tm, name):
    n, k = a.shape
    return pl.pallas_call(
        _mm_kernel,
        grid=(n // tn, width // tm),
        in_specs=[pl.BlockSpec((tn, k), lambda i, j: (i, 0), pipeline_mode=pl.Buffered(1)),
                  pl.BlockSpec((None, k, tm), lambda i, j: (layer, 0, j))],
        out_specs=pl.BlockSpec((tn, tm), lambda i, j: (i, j)),
        out_shape=jax.ShapeDtypeStruct((n, width), F32),
        compiler_params=_params("parallel", "parallel"),
        name=name,
    )(a, w)


def _mm_acc_kernel(a_ref, w_ref, o_ref):
    part = jnp.dot(a_ref[...], w_ref[...].astype(BF16), preferred_element_type=F32)

    @pl.when(pl.program_id(2) == 0)
    def _():
        o_ref[...] = part

    @pl.when(pl.program_id(2) > 0)
    def _():
        o_ref[...] += part


def _matmul_ktiled(a, w, *, layer, tn, tm, tk, name):
    n, k = a.shape
    m = w.shape[2]
    return pl.pallas_call(
        _mm_acc_kernel,
        grid=(n // tn, m // tm, k // tk),
        in_specs=[pl.BlockSpec((tn, tk), lambda i, j, q: (i, q)),
                  pl.BlockSpec((None, tk, tm), lambda i, j, q: (layer, q, j))],
        out_specs=pl.BlockSpec((tn, tm), lambda i, j, q: (i, j)),
        out_shape=jax.ShapeDtypeStruct((n, m), F32),
        compiler_params=_params("parallel", "parallel", "arbitrary"),
        name=name,
    )(a, w)


def _res_ln_kernel(x_ref, y_ref, g_ref, b_ref, o_ref, ob_ref, *, alpha):
    z = alpha * x_ref[...] + y_ref[...]
    mu = jnp.mean(z, axis=-1, keepdims=True)
    zc = z - mu
    var = jnp.mean(zc * zc, axis=-1, keepdims=True)
    out = zc * lax.rsqrt(var + LN_EPS) * g_ref[...] + b_ref[...]
    o_ref[...] = out
    ob_ref[...] = out.astype(BF16)


def _res_ln(x, y, g, b, *, alpha, tn, name):
    n, d = x.shape
    row = pl.BlockSpec((tn, d), lambda i: (i, 0))
    vec = pl.BlockSpec((1, d), lambda i: (0, 0))
    return pl.pallas_call(
        functools.partial(_res_ln_kernel, alpha=alpha),
        grid=(n // tn,),
        in_specs=[row, row, vec, vec],
        out_specs=[row, row],
        out_shape=[jax.ShapeDtypeStruct((n, d), F32),
                   jax.ShapeDtypeStruct((n, d), BF16)],
        compiler_params=_params("parallel"),
        name=name,
    )(x, y, g.reshape(1, d), b.reshape(1, d))


def _conv_kernel(a_ref, gate_ref, prev_ref, cw_ref, cb_ref, lg_ref, lb_ref,
                 y_ref, new_ref, ext_ref, *, tt, n_valid_last, width):
    t = pl.program_id(1)
    keep = width - 1

    @pl.when(t == 0)
    def _():
        ext_ref[0:CONV_HALO, :] = jnp.zeros((CONV_HALO, ext_ref.shape[1]), F32)
        ext_ref[CONV_HALO - keep:CONV_HALO, :] = prev_ref[0]

    ext_ref[CONV_HALO:CONV_HALO + tt, :] = a_ref[...] * jax.nn.sigmoid(gate_ref[...])
    acc = jnp.zeros((tt, ext_ref.shape[1]), F32)
    for w in range(width):
        lo = CONV_HALO - keep + w
        acc = acc + ext_ref[lo:lo + tt, :] * cw_ref[w:w + 1, :]
    y = acc + cb_ref[...]
    mu = jnp.mean(y, axis=-1, keepdims=True)
    yc = y - mu
    var = jnp.mean(yc * yc, axis=-1, keepdims=True)
    y = yc * lax.rsqrt(var + LN_EPS) * lg_ref[...] + lb_ref[...]
    y_ref[...] = (y * jax.nn.sigmoid(y)).astype(y_ref.dtype)

    @pl.when(t == pl.num_programs(1) - 1)
    def _():
        hi = CONV_HALO + n_valid_last
        new_ref[0] = ext_ref[hi - keep:hi, :]

    ext_ref[0:CONV_HALO, :] = ext_ref[tt:tt + CONV_HALO, :]


def _conv_group(proj, prev, cw, cb, lg, lb, *, row0, batches, rows, n_valid, name):
    width, ch = cw.shape
    tt = _pick_tile(rows, 256, 8)
    nt = rows // tt
    blk0 = row0 // tt
    assert row0 % tt == 0 and n_valid > rows - tt
    n_valid_last = n_valid - (nt - 1) * tt
    rowspec = lambda col: pl.BlockSpec((tt, ch), lambda b, t: (blk0 + b * nt + t, col))
    vec = pl.BlockSpec((1, ch), lambda b, t: (0, 0))
    state = pl.BlockSpec((1, width - 1, ch), lambda b, t: (b, 0, 0))
    return pl.pallas_call(
        functools.partial(_conv_kernel, tt=tt, n_valid_last=n_valid_last, width=width),
        grid=(batches, nt),
        in_specs=[rowspec(0), rowspec(1), state,
                  pl.BlockSpec((width, ch), lambda b, t: (0, 0)), vec, vec, vec],
        out_specs=[pl.BlockSpec((tt, ch), lambda b, t: (b * nt + t, 0)), state],
        out_shape=[jax.ShapeDtypeStruct((batches * rows, ch), BF16),
                   jax.ShapeDtypeStruct((batches, width - 1, ch), F32)],
        scratch_shapes=[pltpu.VMEM((CONV_HALO + tt + CONV_HALO, ch), F32)],
        compiler_params=_params("parallel", "arbitrary"),
        name=name,
    )(proj, proj, prev, cw, cb.reshape(1, ch), lg.reshape(1, ch), lb.reshape(1, ch))


def _pattern_count(delta):
    cnt = jnp.zeros(delta.shape, F32)
    for window, dil in DILATED_PATTERNS:
        hit = (delta <= window) & ((delta & (dil - 1)) == 0)
        cnt = cnt + jnp.where(hit, 1.0, 0.0)
    return jnp.where(delta >= 0, cnt, 0.0)


def _attn_prompt_kernel(q_ref, k_ref, v_ref, o_ref, *, qb, scale):
    t_len = q_ref.shape[0]
    kb = k_ref[...].astype(BF16)
    vb = v_ref[...].astype(BF16)
    for i in range(t_len // qb):
        hi = (i + 1) * qb
        q = q_ref[i * qb:hi, :].astype(BF16)
        s = lax.dot_general(q, kb[0:hi], (((1,), (1,)), ((), ())),
                            preferred_element_type=F32) * scale
        qpos = i * qb + lax.broadcasted_iota(jnp.int32, (qb, hi), 0)
        kpos = lax.broadcasted_iota(jnp.int32, (qb, hi), 1)
        cnt = _pattern_count(qpos - kpos)
        s = jnp.where(cnt > 0, s, MASKED)
        mx = jnp.max(s, axis=-1, keepdims=True)
        e = jnp.exp(s - mx) * cnt
        den = jnp.sum(e, axis=-1, keepdims=True)
        out = jnp.dot(e.astype(BF16), vb[0:hi], preferred_element_type=F32)
        o_ref[i * qb:hi, :] = (out / den).astype(o_ref.dtype)


def _attn_prompt(proj, *, batches, t_len, heads, col0, name):
    c0 = col0 // HEAD_DIM
    qb = _pick_tile(t_len, 256, 8)
    spec = lambda which: pl.BlockSpec(
        (t_len, HEAD_DIM), lambda b, h: (b, c0 + which * heads + h))
    return pl.pallas_call(
        functools.partial(_attn_prompt_kernel, qb=qb, scale=HEAD_DIM ** -0.5),
        grid=(batches, heads),
        in_specs=[spec(0), spec(1), spec(2)],
        out_specs=pl.BlockSpec((t_len, HEAD_DIM), lambda b, h: (b, h)),
        out_shape=jax.ShapeDtypeStruct((batches * t_len, heads * HEAD_DIM), BF16),
        compiler_params=_params("parallel", "parallel"),
        name=name,
    )(proj, proj, proj)


def _attn_sample_kernel(q_ref, k_ref, v_ref, kc_ref, vc_ref, o_ref, *, n_valid, scale):
    rows = q_ref.shape[0]
    n_prev = kc_ref.shape[1]
    q = q_ref[...].astype(BF16)
    nt = (((1,), (1,)), ((), ()))
    s_c = lax.dot_general(q, kc_ref[0].astype(BF16), nt, preferred_element_type=F32) * scale
    s_n = lax.dot_general(q, k_ref[...].astype(BF16), nt, preferred_element_type=F32) * scale
    qpos_c = n_prev + lax.broadcasted_iota(jnp.int32, (rows, n_prev), 0)
    cnt_c = _pattern_count(qpos_c - lax.broadcasted_iota(jnp.int32, (rows, n_prev), 1))
    tq = lax.broadcasted_iota(jnp.int32, (rows, rows), 0)
    tk = lax.broadcasted_iota(jnp.int32, (rows, rows), 1)
    cnt_n = jnp.where(tk < n_valid, _pattern_count(tq - tk), 0.0)
    s_c = jnp.where(cnt_c > 0, s_c, MASKED)
    s_n = jnp.where(cnt_n > 0, s_n, MASKED)
    mx = jnp.maximum(jnp.max(s_c, axis=-1, keepdims=True),
                     jnp.max(s_n, axis=-1, keepdims=True))
    e_c = jnp.exp(s_c - mx) * cnt_c
    e_n = jnp.exp(s_n - mx) * cnt_n
    den = jnp.sum(e_c, axis=-1, keepdims=True) + jnp.sum(e_n, axis=-1, keepdims=True)
    out = (jnp.dot(e_c.astype(BF16), vc_ref[0].astype(BF16), preferred_element_type=F32)
           + jnp.dot(e_n.astype(BF16), v_ref[...].astype(BF16), preferred_element_type=F32))
    o_ref[...] = (out / den).astype(o_ref.dtype)


def _attn_sample(proj, cache_k, cache_v, *, layer, row0, batches, heads, col0, n_valid, name):
    c0 = col0 // HEAD_DIM
    r0 = row0 // SAMPLE_ROWS
    n_prev = cache_k.shape[2]
    new = lambda which: pl.BlockSpec(
        (SAMPLE_ROWS, HEAD_DIM), lambda b, h: (r0 + b, c0 + which * heads + h))
    cache = pl.BlockSpec((None, 1, n_prev, HEAD_DIM), lambda b, h: (layer, b, 0, h))
    return pl.pallas_call(
        functools.partial(_attn_sample_kernel, n_valid=n_valid, scale=HEAD_DIM ** -0.5),
        grid=(batches, heads),
        in_specs=[new(0), new(1), new(2), cache, cache],
        out_specs=pl.BlockSpec((SAMPLE_ROWS, HEAD_DIM), lambda b, h: (b, h)),
        out_shape=jax.ShapeDtypeStruct((batches * SAMPLE_ROWS, heads * HEAD_DIM), BF16),
        compiler_params=_params("parallel", "parallel"),
        name=name,
    )(proj, proj, proj, cache_k, cache_v)


def _mlstm_kernel(*refs, chunk, n_chunks, n_valid, hb):
    tok_refs = refs[:4 * hb]
    (gi_ref, gf_ref, bi_ref, bf_ref, ng_ref, c0_ref, n0_ref, m0_ref,
     y_ref, c_ref, n_ref, m_ref) = refs[4 * hb:]
    L = chunk
    row = lax.broadcasted_iota(jnp.int32, (L, L), 0)
    col = lax.broadcasted_iota(jnp.int32, (L, L), 1)
    causal = col <= row
    eye = jnp.where(row == col, 1.0, 0.0)
    upper = jnp.where(row <= col, 1.0, 0.0)
    lower = jnp.where(causal, 1.0, 0.0)
    lane = lax.broadcasted_iota(jnp.int32, (1, L), 1)
    tn = (((0,), (0,)), ((), ()))
    nt = (((1,), (1,)), ((), ()))

    def to_col(r):
        return jnp.sum(eye * r, axis=-1, keepdims=True)

    def one_head(j, c, r0, cs, ns, ms):
        q_ref, k_ref, v_ref, og_ref = tok_refs[4 * j:4 * j + 4]
        q = q_ref[pl.ds(r0, L), :]
        k = k_ref[pl.ds(r0, L), :] * (HEAD_DIM ** -0.5)
        v = v_ref[pl.ds(r0, L), :]
        i_row = gi_ref[j, 0, pl.ds(c, 1), :] + bi_ref[j]
        f_pre = gf_ref[j, 0, pl.ds(c, 1), :] + bf_ref[j]
        f_row = jnp.minimum(f_pre, 0.0) - jnp.log(1.0 + jnp.exp(-jnp.abs(f_pre)))
        if n_valid < L:
            i_row = jnp.where(lane < n_valid, i_row, MASKED)
            f_row = jnp.where(lane < n_valid, f_row, 0.0)
        i_col = to_col(i_row)
        f_col = to_col(f_row)
        b_col = jnp.sum(lower * f_row, axis=-1, keepdims=True)
        b_row = jnp.sum(upper * f_col, axis=0, keepdims=True)
        a_intra = jnp.where(causal, b_col - b_row + i_row, MASKED)
        a_inter = b_col + ms
        m_t = jnp.maximum(a_inter, jnp.max(a_intra, axis=-1, keepdims=True))
        w_intra = jnp.exp(a_intra - m_t)
        w_inter = jnp.exp(a_inter - m_t)
        qb = q.astype(BF16)
        kb = k.astype(BF16)
        vb = v.astype(BF16)
        qk = lax.dot_general(qb, kb, nt, preferred_element_type=F32) * w_intra
        num = (w_inter * jnp.dot(qb, cs.astype(BF16), preferred_element_type=F32)
               + jnp.dot(qk.astype(BF16), vb, preferred_element_type=F32))
        qn = (w_inter * jnp.sum(q * ns, axis=-1, keepdims=True)
              + jnp.sum(qk, axis=-1, keepdims=True))
        h = num / jnp.maximum(jnp.abs(qn), jnp.exp(-m_t))
        m_new = m_t[L - 1:L, :]
        b_last = b_col[L - 1:L, :]
        decay = jnp.exp(b_last + ms - m_new)
        g_col = jnp.exp(b_last - b_col + i_col - m_new)
        kg = k * g_col
        c_new = decay * cs + lax.dot_general(kg.astype(BF16), vb, tn,
                                             preferred_element_type=F32)
        n_new = decay * ns + jnp.sum(kg, axis=0, keepdims=True)
        hg = jax.nn.sigmoid(og_ref[pl.ds(r0, L), :]) * h
        mu = jnp.mean(hg, axis=-1, keepdims=True)
        hc = hg - mu
        var = jnp.mean(hc * hc, axis=-1, keepdims=True)
        y_ref[pl.ds(r0, L), j * HEAD_DIM:(j + 1) * HEAD_DIM] = (
            hc * lax.rsqrt(var + LN_EPS) * ng_ref[j]).astype(y_ref.dtype)
        return c_new, n_new, m_new

    def body(c, carry):
        r0 = pl.multiple_of(c * L, L)
        out = []
        for j in range(hb):
            out.extend(one_head(j, c, r0, *carry[3 * j:3 * j + 3]))
        return tuple(out)

    init = []
    for j in range(hb):
        init.extend((c0_ref[0, j], n0_ref[0, j], m0_ref[0, j]))
    final = lax.fori_loop(0, n_chunks, body, tuple(init),
                          unroll=2 if n_chunks % 2 == 0 else 1)
    for j in range(hb):
        c_ref[0, j] = final[3 * j]
        n_ref[0, j] = final[3 * j + 1]
        m_ref[0, j] = final[3 * j + 2]


def _mlstm_group(proj, gates_t, b_i, b_f, norm_g, c0, n0, m0,
                 *, row0, batches, rows, n_valid, col0, name):
    heads = b_i.shape[0]
    hb = max(g for g in (4, 3, 2, 1) if heads % g == 0)
    chunk = MLSTM_CHUNK if n_valid % MLSTM_CHUNK == 0 else rows
    assert n_valid == rows or chunk == rows
    n_chunks = rows // chunk
    c0b = col0 // HEAD_DIM
    rb = row0 // rows
    g4 = gates_t[:, row0:row0 + batches * rows].reshape(2 * heads, batches, n_chunks, chunk)

    def tok(which, j):
        return pl.BlockSpec((rows, HEAD_DIM),
                            lambda b, h: (rb + b, c0b + which * heads + h * hb + j))

    gate = lambda off: pl.BlockSpec(
        (hb, 1, n_chunks, chunk), lambda b, h: (off // hb + h, b, 0, 0))
    scal = pl.BlockSpec((hb, 1, 1), lambda b, h: (h, 0, 0))
    cspec = pl.BlockSpec((1, hb, HEAD_DIM, HEAD_DIM), lambda b, h: (b, h, 0, 0))
    nspec = pl.BlockSpec((1, hb, 1, HEAD_DIM), lambda b, h: (b, h, 0, 0))
    mspec = pl.BlockSpec((1, hb, 1, 1), lambda b, h: (b, h, 0, 0))
    tok_specs = [tok(which, j) for j in range(hb) for which in range(4)]
    y, c, n, m = pl.pallas_call(
        functools.partial(_mlstm_kernel, chunk=chunk, n_chunks=n_chunks,
                          n_valid=min(n_valid, chunk), hb=hb),
        grid=(batches, heads // hb),
        in_specs=tok_specs + [gate(0), gate(heads), scal, scal,
                              pl.BlockSpec((hb, 1, HEAD_DIM), lambda b, h: (h, 0, 0)),
                              cspec, nspec, mspec],
        out_specs=[pl.BlockSpec((rows, hb * HEAD_DIM), lambda b, h: (b, h)), cspec, nspec, mspec],
        out_shape=[jax.ShapeDtypeStruct((batches * rows, heads * HEAD_DIM), BF16),
                   jax.ShapeDtypeStruct(c0.shape, F32),
                   jax.ShapeDtypeStruct(n0.shape[:2] + (1, HEAD_DIM), F32),
                   jax.ShapeDtypeStruct(m0.shape + (1, 1), F32)],
        compiler_params=_params("parallel", "parallel"),
        name=name,
    )(*([proj] * (4 * hb)), g4, g4,
      b_i.reshape(heads, 1, 1), b_f.reshape(heads, 1, 1), norm_g.reshape(heads, 1, HEAD_DIM),
      c0, n0.reshape(n0.shape[:2] + (1, HEAD_DIM)), m0.reshape(m0.shape + (1, 1)))
    return y, c, n.reshape(n0.shape), m.reshape(m0.shape)


def _topk_rows(vals, order, k, payload=None):
    out_v, out_p = [], []
    for _ in range(k):
        m = jnp.max(vals, axis=0, keepdims=True)
        first = jnp.min(jnp.where(vals == m, order, float(2 ** 20)), axis=0, keepdims=True)
        hit = order == first
        out_v.append(m)
        if payload is None:
            out_p.append(first)
        else:
            out_p.append(jnp.sum(jnp.where(hit, payload, 0.0), axis=0, keepdims=True))
        vals = jnp.where(hit, -jnp.inf, vals)
    return jnp.concatenate(out_v, axis=0), jnp.concatenate(out_p, axis=0)


def _candidate_pieces(k):
    split = 5
    valid = lambda a, b: (a + 1) * (b + 1) <= k
    pieces = []
    for b in range(min(split, k)):
        for a0 in range(0, k, 8):
            if any(valid(a, b) for a in range(a0, a0 + 8)):
                pieces.append(("col", a0, b))
    for a in range(k):
        for b0 in range(0, k, 8):
            if any(valid(a, b) and b >= split for b in range(b0, b0 + 8)):
                pieces.append(("row", a, b0))
    covered = []
    for kind, p, q in pieces:
        for r in range(8):
            a, b = (p + r, q) if kind == "col" else (p, q + r)
            if valid(a, b) and (kind == "col" or b >= split):
                covered.append((a, b))
    assert sorted(covered) == sorted((a, b) for a in range(k) for b in range(k) if valid(a, b))
    return pieces, split


def _pair_candidates(sv, si, n_keys, k):
    tok = sv[0].shape[1]
    pieces, split = _candidate_pieces(k)
    r8 = lax.broadcasted_iota(jnp.int32, (8, tok), 0)
    vals, order, eid = [], [], []
    for piece, (kind, p, q) in enumerate(pieces):
        if kind == "col":
            a_idx, b_idx = p + r8, jnp.full((8, tok), q, jnp.int32)
            v = sv[0][p:p + 8] + sv[1][q:q + 1]
            e = si[0][p:p + 8] * n_keys + si[1][q:q + 1]
            ok = (a_idx + 1) * (q + 1) <= k
        else:
            a_idx, b_idx = jnp.full((8, tok), p, jnp.int32), q + r8
            v = sv[0][p:p + 1] + sv[1][q:q + 8]
            e = si[0][p:p + 1] * n_keys + si[1][q:q + 8]
            ok = ((p + 1) * (b_idx + 1) <= k) & (b_idx >= split)
        vals.append(jnp.where(ok, v, -jnp.inf))
        unused = k * k + piece * 8 + r8
        order.append(jnp.where(ok, a_idx * k + b_idx, unused).astype(F32))
        eid.append(e)
    return (jnp.concatenate(vals, axis=0), jnp.concatenate(order, axis=0),
            jnp.concatenate(eid, axis=0))


def _route_kernel(q_ref, sk_ref, g_ref, it_ref, jt_ref, gt_ref, w_ref, *, heads, n_keys):
    tok = q_ref.shape[0]
    half = sk_ref.shape[-1]
    K = PEER_TOPK
    nt = (((1,), (1,)), ((), ()))
    key_id = lax.broadcasted_iota(jnp.int32, (n_keys, tok), 0).astype(F32)
    i_rows, j_rows, g_rows = [], [], []
    for h in range(heads):
        sv, si = [], []
        for p in range(2):
            c = (h * 2 + p) * half
            qh = q_ref[:, c:c + half].astype(BF16)
            st = lax.dot_general(sk_ref[h, p], qh, nt, preferred_element_type=F32)
            v, i = _topk_rows(st, key_id, K)
            sv.append(v)
            si.append(i)
        cand, flat, cid = _pair_candidates(sv, si, n_keys, K)
        cv, eid = _topk_rows(cand, flat, K, payload=cid)
        e = jnp.exp(cv - cv[0:1])
        gate = e / jnp.sum(e, axis=0, keepdims=True)
        ei = jnp.floor(eid * (1.0 / n_keys))
        i_rows.append(ei)
        j_rows.append(eid - ei * n_keys)
        g_rows.append(gate)
    it_ref[...] = jnp.concatenate(i_rows, axis=0).T
    jt_ref[...] = jnp.concatenate(j_rows, axis=0).T
    gt_ref[...] = jnp.concatenate(g_rows, axis=0).T

    picks = heads * K
    sub = lax.broadcasted_iota(jnp.int32, (n_keys, picks), 0).astype(F32)

    def per_token(n, _):
        irow = it_ref[pl.ds(n, 1), :]
        jrow = jt_ref[pl.ds(n, 1), :]
        grow = gt_ref[pl.ds(n, 1), :]
        a_t = jnp.where(sub == irow, 1.0, 0.0).astype(BF16)
        b_t = jnp.where(sub == jrow, grow, 0.0).astype(BF16)
        r0 = pl.multiple_of(n * n_keys, n_keys)
        w_ref[pl.ds(r0, n_keys), :] = lax.dot_general(a_t, b_t, nt, preferred_element_type=F32)
        return 0

    lax.fori_loop(0, tok, per_token, 0, unroll=ROUTE_UNROLL)
    for i in range(n_keys):
        g_ref[:, i * n_keys:(i + 1) * n_keys] = w_ref[pl.ds(i, tok, stride=n_keys), :].astype(g_ref.dtype)


def _route(qp, subkeys_bf, *, name):
    n = qp.shape[0]
    heads, _, n_keys, half = subkeys_bf.shape
    tok = LANES
    picks = heads * PEER_TOPK
    return pl.pallas_call(
        functools.partial(_route_kernel, heads=heads, n_keys=n_keys),
        grid=(n // tok,),
        in_specs=[pl.BlockSpec((tok, qp.shape[1]), lambda i: (i, 0)),
                  pl.BlockSpec(subkeys_bf.shape, lambda i: (0, 0, 0, 0))],
        out_specs=pl.BlockSpec((tok, n_keys * n_keys), lambda i: (i, 0)),
        out_shape=jax.ShapeDtypeStruct((n, n_keys * n_keys), BF16),
        scratch_shapes=[pltpu.VMEM((tok, picks), F32)] * 3
        + [pltpu.VMEM((tok * n_keys, n_keys), F32)],
        compiler_params=_params("parallel"),
        name=name,
    )(qp, subkeys_bf)


def _peer_act_kernel(x_ref, u_ref, g_ref, p_ref):
    s = lax.dot_general(x_ref[...], u_ref[...].astype(BF16), (((1,), (1,)), ((), ())),
                        preferred_element_type=F32)
    act = 0.5 * s * (1.0 + lax.erf(s * (1.0 / math.sqrt(2.0))))
    p_ref[...] = (act * g_ref[...].astype(F32)).astype(p_ref.dtype)


def _peer_act(xb, u, g, *, layer, tn, te, name):
    n, d = xb.shape
    n_exp = u.shape[1]
    return pl.pallas_call(
        _peer_act_kernel,
        grid=(n // tn, n_exp // te),
        in_specs=[pl.BlockSpec((tn, d), lambda i, e: (i, 0), pipeline_mode=pl.Buffered(1)),
                  pl.BlockSpec((None, te, d), lambda i, e: (layer, e, 0)),
                  pl.BlockSpec((tn, te), lambda i, e: (i, e))],
        out_specs=pl.BlockSpec((tn, te), lambda i, e: (i, e)),
        out_shape=jax.ShapeDtypeStruct((n, n_exp), BF16),
        compiler_params=_params("parallel", "parallel"),
        name=name,
    )(xb, u, g)


def _merge_heads_kernel(k_ref, v_ref, ko_ref, vo_ref):
    for src, dst in ((k_ref, ko_ref), (v_ref, vo_ref)):
        for h in range(src.shape[1]):
            dst[:, h * HEAD_DIM:(h + 1) * HEAD_DIM] = src[:, h, :]


def _merge_cache_heads(cache_k, cache_v, *, name):
    depth, bs, n_prev, heads, hd = cache_k.shape
    tt = _pick_tile(n_prev, 256, 8)
    src = pl.BlockSpec((None, None, tt, heads, hd), lambda l, b, t: (l, b, t, 0, 0))
    dst = pl.BlockSpec((None, None, tt, heads * hd), lambda l, b, t: (l, b, t, 0))
    shape = jax.ShapeDtypeStruct((depth, bs, n_prev, heads * hd), cache_k.dtype)
    return pl.pallas_call(
        _merge_heads_kernel,
        grid=(depth, bs, n_prev // tt),
        in_specs=[src, src],
        out_specs=[dst, dst],
        out_shape=[shape, shape],
        compiler_params=_params("parallel", "parallel", "parallel"),
        name=name,
    )(cache_k, cache_v)


def _split_heads_kernel(*refs, depth, pieces):
    ins, (ko_ref, vo_ref) = refs[:-2], refs[-2:]
    per = ins[0].shape[1] // HEAD_DIM
    for l in range(depth):
        for t, dst in enumerate((ko_ref, vo_ref)):
            for c in range(pieces):
                src = ins[(l * 2 + t) * pieces + c]
                for hh in range(per):
                    dst[l, :, c * per + hh, :] = src[:, hh * HEAD_DIM:(hh + 1) * HEAD_DIM]


def _split_kv_heads(projs, *, rows, heads, col_k, col_v, name):
    depth = len(projs)
    width = math.gcd(heads * HEAD_DIM, math.gcd(col_k, col_v))
    pieces = heads * HEAD_DIM // width
    tt = _pick_tile(rows, 256, 8)
    specs, args = [], []
    for proj in projs:
        for col in (col_k, col_v):
            for c in range(pieces):
                specs.append(pl.BlockSpec((tt, width), lambda i, cb=col // width + c: (i, cb)))
                args.append(proj)
    dst = pl.BlockSpec((depth, tt, heads, HEAD_DIM), lambda i: (0, i, 0, 0))
    shape = jax.ShapeDtypeStruct((depth, rows, heads, HEAD_DIM), F32)
    return pl.pallas_call(
        functools.partial(_split_heads_kernel, depth=depth, pieces=pieces),
        grid=(rows // tt,),
        in_specs=specs,
        out_specs=[dst, dst],
        out_shape=[shape, shape],
        compiler_params=_params("parallel"),
        name=name,
    )(*args)


def _layer(x, xb, st, w, big, *, layer, dims, alpha):
    bp, tp, bs, ts = dims
    tag = str(layer)
    n, d = x.shape
    rows_p = bp * tp
    conv_ch = w["conv_w"].shape[-1]
    a_heads = big["cache_k"].shape[3] // HEAD_DIM
    m_heads = w["b_i"].shape[0]
    attn_w = a_heads * HEAD_DIM
    mlstm_w = m_heads * HEAD_DIM
    off_att = 2 * conv_ch
    off_mls = off_att + 3 * attn_w
    off_gate = off_mls + 4 * mlstm_w

    tn_big = _pick_tile(n, 2080, SUBLANES_BF16)
    tn_acc = _pick_tile(n, 1664, SUBLANES_BF16)
    proj = _matmul(xb, big["w_in"], layer=layer, width=off_gate, tn=tn_big,
                   tm=_pick_tile(off_gate, 512, LANES), name=f"proj{tag}")
    gates = _matmul(xb, big["w_gate"], layer=layer, width=LANES, tn=tn_big, tm=LANES,
                    name=f"gates{tag}")
    gates_t = gates[:, :2 * m_heads].T

    zeros = lambda *s: jnp.zeros(s, F32)
    conv_args = (w["conv_w"], w["conv_b"], w["conv_ln_g"], w["conv_ln_b"])
    ya_p, conv_p = _conv_group(proj, zeros(bp, w["conv_w"].shape[0] - 1, conv_ch), *conv_args,
                               row0=0, batches=bp, rows=tp, n_valid=tp, name=f"conv_p{tag}")
    ya_s, conv_s = _conv_group(proj, st["conv"], *conv_args, row0=rows_p, batches=bs,
                               rows=SAMPLE_ROWS, n_valid=ts, name=f"conv_s{tag}")

    yb_p = _attn_prompt(proj, batches=bp, t_len=tp, heads=a_heads, col0=off_att,
                        name=f"attn_p{tag}")
    yb_s = _attn_sample(proj, big["cache_k"], big["cache_v"], layer=layer, row0=rows_p,
                        batches=bs, heads=a_heads, col0=off_att, n_valid=ts, name=f"attn_s{tag}")

    ml_args = (w["b_i"], w["b_f"], w["norm_g"])
    yc_p, c_p, n_p, m_p = _mlstm_group(
        proj, gates_t, *ml_args, zeros(bp, m_heads, HEAD_DIM, HEAD_DIM),
        zeros(bp, m_heads, HEAD_DIM), zeros(bp, m_heads),
        row0=0, batches=bp, rows=tp, n_valid=tp, col0=off_mls, name=f"mlstm_p{tag}")
    yc_s, c_s, n_s, m_s = _mlstm_group(
        proj, gates_t, *ml_args, st["c"], st["n"], st["m"],
        row0=rows_p, batches=bs, rows=SAMPLE_ROWS, n_valid=ts, col0=off_mls,
        name=f"mlstm_s{tag}")

    rows_s = bs * SAMPLE_ROWS
    ymix = jnp.concatenate([
        jnp.concatenate([ya_p, yb_p, yc_p], axis=1),
        jnp.concatenate([ya_s, yb_s, yc_s], axis=1),
        jnp.zeros((n - rows_p - rows_s, d), BF16)], axis=0)
    mix = _matmul(ymix, big["w_out"], layer=layer, width=d, tn=tn_big,
                  tm=_pick_tile(d, 512, LANES), name=f"outproj{tag}")
    tn_ln = _pick_tile(n, 256, SUBLANES_BF16)
    x1, x1b = _res_ln(x, mix, w["ln1_g"], w["ln1_b"], alpha=alpha, tn=tn_ln, name=f"ln1{tag}")

    q_dim = big["peer_wq"].shape[2]
    qp = _matmul(x1b, big["peer_wq"], layer=layer, width=q_dim, tn=tn_big,
                 tm=_pick_tile(q_dim, 512, LANES), name=f"peer_q{tag}")
    g = _route(qp, w["peer_subkeys"], name=f"route{tag}")
    n_exp = big["peer_u"].shape[1]
    p = _peer_act(x1b, big["peer_u"], g, layer=layer, tn=tn_big,
                  te=_pick_tile(n_exp, 512, LANES), name=f"peer_act{tag}")
    po = _matmul_ktiled(p, big["peer_v"], layer=layer, tn=tn_acc, tm=_pick_tile(d, 1024, LANES),
                        tk=_pick_tile(n_exp, 2048, LANES), name=f"peer_out{tag}")
    x2, x2b = _res_ln(x1, po, w["ln2_g"], w["ln2_b"], alpha=alpha, tn=tn_ln, name=f"ln2{tag}")

    kv = lambda rows, b, t, which: rows[:, off_att + which * attn_w:off_att + (which + 1) * attn_w
                                        ].reshape(b, t, a_heads, HEAD_DIM)
    sr = proj[rows_p:rows_p + rows_s]
    outs = dict(
        proj=proj, kv_cols=(off_att + attn_w, off_att + 2 * attn_w),
        k_s=kv(sr, bs, SAMPLE_ROWS, 1)[:, :ts], v_s=kv(sr, bs, SAMPLE_ROWS, 2)[:, :ts],
        conv_p=conv_p, conv_s=conv_s, c_p=c_p, c_s=c_s, n_p=n_p, n_s=n_s, m_p=m_p, m_s=m_s)
    return x2, x2b, outs


def kernel(x_prompt, x_sample, cache_attn_k, cache_attn_v, state_conv, state_mlstm_c, state_mlstm_n, state_mlstm_m, w_in, conv_w, conv_b, conv_ln_g, conv_ln_b, mlstm_b_i, mlstm_b_f, mlstm_norm_g, w_out, ln1_g, ln1_b, peer_wq, peer_subkeys, peer_u, peer_v, ln2_g, ln2_b):
    bp, tp, d = x_prompt.shape
    bs, ts, _ = x_sample.shape
    depth = w_in.shape[0]
    assert ts <= SAMPLE_ROWS and tp % SAMPLE_ROWS == 0
    rows_p = bp * tp
    rows_s = bs * SAMPLE_ROWS
    n = -(-(rows_p + rows_s) // LANES) * LANES
    xs = jnp.pad(x_sample, ((0, 0), (0, SAMPLE_ROWS - ts), (0, 0))).reshape(rows_s, d)
    x = jnp.concatenate([x_prompt.reshape(rows_p, d), xs,
                         jnp.zeros((n - rows_p - rows_s, d), x_prompt.dtype)], axis=0)
    xb = x.astype(BF16)
    alpha = (2 * depth) ** 0.25

    n_gate = 2 * mlstm_b_i.shape[1]
    w_in_b = w_in.astype(BF16)
    cache_k, cache_v = _merge_cache_heads(cache_attn_k, cache_attn_v, name="cache_heads")
    big = dict(
        w_in=w_in_b,
        w_gate=jnp.pad(w_in_b[:, :, w_in.shape[2] - n_gate:], ((0, 0), (0, 0), (0, LANES - n_gate))),
        w_out=w_out.astype(BF16), peer_wq=peer_wq.astype(BF16), peer_u=peer_u, peer_v=peer_v,
        cache_k=cache_k, cache_v=cache_v)
    subkeys_b = peer_subkeys.astype(BF16)

    per_layer = []
    for l in range(depth):
        st = dict(conv=state_conv[l], c=state_mlstm_c[l], n=state_mlstm_n[l], m=state_mlstm_m[l])
        w = dict(conv_w=conv_w[l], conv_b=conv_b[l], conv_ln_g=conv_ln_g[l],
                 conv_ln_b=conv_ln_b[l], b_i=mlstm_b_i[l], b_f=mlstm_b_f[l],
                 norm_g=mlstm_norm_g[l], ln1_g=ln1_g[l], ln1_b=ln1_b[l],
                 peer_subkeys=subkeys_b[l], ln2_g=ln2_g[l], ln2_b=ln2_b[l])
        x, xb, outs = _layer(x, xb, st, w, big, layer=l, dims=(bp, tp, bs, ts), alpha=alpha)
        per_layer.append(outs)

    a_heads = cache_attn_k.shape[3]
    col_k, col_v = per_layer[0]["kv_cols"]
    k_all, v_all = _split_kv_heads([o["proj"] for o in per_layer], rows=rows_p, heads=a_heads,
                                   col_k=col_k, col_v=col_v, name="kv_prompt")
    win = min(max(wd for wd, _ in DILATED_PATTERNS), tp)
    k_prompt = k_all.reshape(depth, bp, tp, a_heads, HEAD_DIM)[:, :, tp - win:]
    v_prompt = v_all.reshape(depth, bp, tp, a_heads, HEAD_DIM)[:, :, tp - win:]

    stack = lambda key: jnp.stack([o[key] for o in per_layer])
    y_prompt = x[:rows_p].reshape(bp, tp, d)
    y_sample = x[rows_p:rows_p + rows_s].reshape(bs, SAMPLE_ROWS, d)[:, :ts]
    return (y_prompt, y_sample, k_prompt, v_prompt, stack("k_s"), stack("v_s"),
            stack("conv_p"), stack("conv_s"), stack("c_p"), stack("c_s"),
            stack("n_p"), stack("n_s"), stack("m_p"), stack("m_s"))
```

```python
import functools
import math

import jax
import jax.numpy as jnp
from jax import lax
from jax.experimental import pallas as pl
from jax.experimental.pallas import tpu as pltpu

F32 = jnp.float32
BF16 = jnp.bfloat16

LANES = 128
SUBLANES_BF16 = 16
VMEM_LIMIT_BYTES = 56 * 1024 * 1024

HEAD_DIM = 128
DILATED_PATTERNS = ((128, 1), (512, 4), (2048, 16))
MLSTM_CHUNK = 64
PEER_TOPK = 16
LN_EPS = 1e-5
SAMPLE_ROWS = 16
CONV_HALO = 32
MASKED = -1e30
ROUTE_UNROLL = 8


def _params(*sem):
    return pltpu.CompilerParams(dimension_semantics=sem,
                                vmem_limit_bytes=VMEM_LIMIT_BYTES)


def _pick_tile(n, target, align):
    best = None
    for t in range(align, min(n, target) + 1, align):
        if n % t == 0:
            best = t
    assert best is not None, (n, target, align)
    return best


def _mm_kernel(a_ref, w_ref, o_ref, *, transposed):
    dims = (((1,), (1,)), ((), ())) if transposed else (((1,), (0,)), ((), ()))
    o_ref[...] = lax.dot_general(a_ref[...], w_ref[...].astype(BF16), dims,
                                 preferred_element_type=F32)


def _matmul(a, w, *, layer, width, tn, tm, name, transposed=False):
    n, k = a.shape
    if transposed:
        w_spec = pl.BlockSpec((None, tm, k), lambda i, j: (layer, j, 0))
    else:
        w_spec = pl.BlockSpec((None, k, tm), lambda i, j: (layer, 0, j))
    return pl.pallas_call(
        functools.partial(_mm_kernel, transposed=transposed),
        grid=(n // tn, width // tm),
        in_specs=[pl.BlockSpec((tn, k), lambda i, j: (i, 0), pipeline_mode=pl.Buffered(1)),
                  w_spec],
        out_specs=pl.BlockSpec((tn, tm), lambda i, j: (i, j)),
        out_shape=jax.ShapeDtypeStruct((n, width), F32),
        compiler_params=_params("parallel", "parallel"),
        name=name,
    )(a, w)


def _mm_acc_kernel(a_ref, w_ref, o_ref):
    part = jnp.dot(a_ref[...], w_ref[...].astype(BF16), preferred_element_type=F32)

    @pl.when(pl.program_id(2) == 0)
    def _():
        o_ref[...] = part

    @pl.when(pl.program_id(2) > 0)
    def _():
        o_ref[...] += part


def _matmul_ktiled(a, w, *, layer, tn, tm, tk, name):
    n, k = a.shape
    m = w.shape[2]
    return pl.pallas_call(
        _mm_acc_kernel,
        grid=(n // tn, m // tm, k // tk),
        in_specs=[pl.BlockSpec((tn, tk), lambda i, j, q: (i, q)),
                  pl.BlockSpec((None, tk, tm), lambda i, j, q: (layer, q, j))],
        out_specs=pl.BlockSpec((tn, tm), lambda i, j, q: (i, j)),
        out_shape=jax.ShapeDtypeStruct((n, m), F32),
        compiler_params=_params("parallel", "parallel", "arbitrary"),
        name=name,
    )(a, w)


def _res_ln_kernel(x_ref, y_ref, g_ref, b_ref, o_ref, ob_ref, *, alpha):
    z = alpha * x_ref[...] + y_ref[...]
    mu = jnp.mean(z, axis=-1, keepdims=True)
    zc = z - mu
    var = jnp.mean(zc * zc, axis=-1, keepdims=True)
    out = zc * lax.rsqrt(var + LN_EPS) * g_ref[...] + b_ref[...]
    o_ref[...] = out
    ob_ref[...] = out.astype(BF16)


def _res_ln(x, y, g, b, *, alpha, tn, name):
    n, d = x.shape
    row = pl.BlockSpec((tn, d), lambda i: (i, 0))
    vec = pl.BlockSpec((1, d), lambda i: (0, 0))
    return pl.pallas_call(
        functools.partial(_res_ln_kernel, alpha=alpha),
        grid=(n // tn,),
        in_specs=[row, row, vec, vec],
        out_specs=[row, row],
        out_shape=[jax.ShapeDtypeStruct((n, d), F32),
                   jax.ShapeDtypeStruct((n, d), BF16)],
        compiler_params=_params("parallel"),
        name=name,
    )(x, y, g.reshape(1, d), b.reshape(1, d))


def _conv_kernel(a_ref, gate_ref, prev_ref, cw_ref, cb_ref, lg_ref, lb_ref,
                 y_ref, new_ref, ext_ref, *, tt, n_valid_last, width):
    t = pl.program_id(1)
    keep = width - 1

    @pl.when(t == 0)
    def _():
        ext_ref[0:CONV_HALO, :] = jnp.zeros((CONV_HALO, ext_ref.shape[1]), F32)
        ext_ref[CONV_HALO - keep:CONV_HALO, :] = prev_ref[0]

    ext_ref[CONV_HALO:CONV_HALO + tt, :] = a_ref[...] * jax.nn.sigmoid(gate_ref[...])
    acc = jnp.zeros((tt, ext_ref.shape[1]), F32)
    for w in range(width):
        lo = CONV_HALO - keep + w
        acc = acc + ext_ref[lo:lo + tt, :] * cw_ref[w:w + 1, :]
    y = acc + cb_ref[...]
    mu = jnp.mean(y, axis=-1, keepdims=True)
    yc = y - mu
    var = jnp.mean(yc * yc, axis=-1, keepdims=True)
    y = yc * lax.rsqrt(var + LN_EPS) * lg_ref[...] + lb_ref[...]
    y_ref[...] = (y * jax.nn.sigmoid(y)).astype(y_ref.dtype)

    @pl.when(t == pl.num_programs(1) - 1)
    def _():
        hi = CONV_HALO + n_valid_last
        new_ref[0] = ext_ref[hi - keep:hi, :]

    ext_ref[0:CONV_HALO, :] = ext_ref[tt:tt + CONV_HALO, :]


def _conv_group(proj, prev, cw, cb, lg, lb, *, row0, batches, rows, n_valid, name):
    width, ch = cw.shape
    tt = _pick_tile(rows, 256, 8)
    nt = rows // tt
    blk0 = row0 // tt
    assert row0 % tt == 0 and n_valid > rows - tt
    n_valid_last = n_valid - (nt - 1) * tt
    rowspec = lambda col: pl.BlockSpec((tt, ch), lambda b, t: (blk0 + b * nt + t, col))
    vec = pl.BlockSpec((1, ch), lambda b, t: (0, 0))
    state = pl.BlockSpec((1, width - 1, ch), lambda b, t: (b, 0, 0))
    return pl.pallas_call(
        functools.partial(_conv_kernel, tt=tt, n_valid_last=n_valid_last, width=width),
        grid=(batches, nt),
        in_specs=[rowspec(0), rowspec(1), state,
                  pl.BlockSpec((width, ch), lambda b, t: (0, 0)), vec, vec, vec],
        out_specs=[pl.BlockSpec((tt, ch), lambda b, t: (b * nt + t, 0)), state],
        out_shape=[jax.ShapeDtypeStruct((batches * rows, ch), BF16),
                   jax.ShapeDtypeStruct((batches, width - 1, ch), F32)],
        scratch_shapes=[pltpu.VMEM((CONV_HALO + tt + CONV_HALO, ch), F32)],
        compiler_params=_params("parallel", "arbitrary"),
        name=name,
    )(proj, proj, prev, cw, cb.reshape(1, ch), lg.reshape(1, ch), lb.reshape(1, ch))


def _pattern_count(delta):
    cnt = jnp.zeros(delta.shape, F32)
    for window, dil in DILATED_PATTERNS:
        hit = (delta <= window) & ((delta & (dil - 1)) == 0)
        cnt = cnt + jnp.where(hit, 1.0, 0.0)
    return jnp.where(delta >= 0, cnt, 0.0)


def _attn_prompt_kernel(q_ref, k_ref, v_ref, o_ref, *, qb, scale):
    t_len = q_ref.shape[0]
    kb = k_ref[...].astype(BF16)
    vb = v_ref[...].astype(BF16)
    for i in range(t_len // qb):
        hi = (i + 1) * qb
        q = q_ref[i * qb:hi, :].astype(BF16)
        s = lax.dot_general(q, kb[0:hi], (((1,), (1,)), ((), ())),
                            preferred_element_type=F32) * scale
        qpos = i * qb + lax.broadcasted_iota(jnp.int32, (qb, hi), 0)
        kpos = lax.broadcasted_iota(jnp.int32, (qb, hi), 1)
        cnt = _pattern_count(qpos - kpos)
        s = jnp.where(cnt > 0, s, MASKED)
        mx = jnp.max(s, axis=-1, keepdims=True)
        e = jnp.exp(s - mx) * cnt
        den = jnp.sum(e, axis=-1, keepdims=True)
        out = jnp.dot(e.astype(BF16), vb[0:hi], preferred_element_type=F32)
        o_ref[i * qb:hi, :] = (out / den).astype(o_ref.dtype)


def _attn_prompt(proj, *, batches, t_len, heads, col0, name):
    c0 = col0 // HEAD_DIM
    qb = _pick_tile(t_len, 256, 8)
    spec = lambda which: pl.BlockSpec(
        (t_len, HEAD_DIM), lambda b, h: (b, c0 + which * heads + h))
    return pl.pallas_call(
        functools.partial(_attn_prompt_kernel, qb=qb, scale=HEAD_DIM ** -0.5),
        grid=(batches, heads),
        in_specs=[spec(0), spec(1), spec(2)],
        out_specs=pl.BlockSpec((t_len, HEAD_DIM), lambda b, h: (b, h)),
        out_shape=jax.ShapeDtypeStruct((batches * t_len, heads * HEAD_DIM), BF16),
        compiler_params=_params("parallel", "parallel"),
        name=name,
    )(proj, proj, proj)


def _attn_sample_kernel(q_ref, k_ref, v_ref, kc_ref, vc_ref, o_ref, *, n_valid, scale):
    rows = q_ref.shape[0]
    n_prev = kc_ref.shape[0]
    q = q_ref[...].astype(BF16)
    nt = (((1,), (1,)), ((), ()))
    s_c = lax.dot_general(q, kc_ref[...].astype(BF16), nt, preferred_element_type=F32) * scale
    s_n = lax.dot_general(q, k_ref[...].astype(BF16), nt, preferred_element_type=F32) * scale
    qpos_c = n_prev + lax.broadcasted_iota(jnp.int32, (rows, n_prev), 0)
    cnt_c = _pattern_count(qpos_c - lax.broadcasted_iota(jnp.int32, (rows, n_prev), 1))
    tq = lax.broadcasted_iota(jnp.int32, (rows, rows), 0)
    tk = lax.broadcasted_iota(jnp.int32, (rows, rows), 1)
    cnt_n = jnp.where(tk < n_valid, _pattern_count(tq - tk), 0.0)
    s_c = jnp.where(cnt_c > 0, s_c, MASKED)
    s_n = jnp.where(cnt_n > 0, s_n, MASKED)
    mx = jnp.maximum(jnp.max(s_c, axis=-1, keepdims=True),
                     jnp.max(s_n, axis=-1, keepdims=True))
    e_c = jnp.exp(s_c - mx) * cnt_c
    e_n = jnp.exp(s_n - mx) * cnt_n
    den = jnp.sum(e_c, axis=-1, keepdims=True) + jnp.sum(e_n, axis=-1, keepdims=True)
    out = (jnp.dot(e_c.astype(BF16), vc_ref[...].astype(BF16), preferred_element_type=F32)
           + jnp.dot(e_n.astype(BF16), v_ref[...].astype(BF16), preferred_element_type=F32))
    o_ref[...] = (out / den).astype(o_ref.dtype)


def _attn_sample(proj, cache_k, cache_v, *, layer, row0, batches, heads, col0, n_valid, name):
    c0 = col0 // HEAD_DIM
    r0 = row0 // SAMPLE_ROWS
    n_prev = cache_k.shape[3]
    new = lambda which: pl.BlockSpec(
        (SAMPLE_ROWS, HEAD_DIM), lambda b, h: (r0 + b, c0 + which * heads + h))
    cache = pl.BlockSpec((None, None, None, n_prev, HEAD_DIM), lambda b, h: (layer, b, h, 0, 0))
    return pl.pallas_call(
        functools.partial(_attn_sample_kernel, n_valid=n_valid, scale=HEAD_DIM ** -0.5),
        grid=(batches, heads),
        in_specs=[new(0), new(1), new(2), cache, cache],
        out_specs=pl.BlockSpec((SAMPLE_ROWS, HEAD_DIM), lambda b, h: (b, h)),
        out_shape=jax.ShapeDtypeStruct((batches * SAMPLE_ROWS, heads * HEAD_DIM), BF16),
        compiler_params=_params("parallel", "parallel"),
        name=name,
    )(proj, proj, proj, cache_k, cache_v)


def _mlstm_kernel(*refs, chunk, n_chunks, n_valid, hb):
    tok_refs = refs[:4 * hb]
    (gi_ref, gf_ref, bi_ref, bf_ref, ng_ref, c0_ref, n0_ref, m0_ref,
     y_ref, c_ref, n_ref, m_ref) = refs[4 * hb:]
    L = chunk
    row = lax.broadcasted_iota(jnp.int32, (L, L), 0)
    col = lax.broadcasted_iota(jnp.int32, (L, L), 1)
    causal = col <= row
    eye = jnp.where(row == col, 1.0, 0.0)
    upper = jnp.where(row <= col, 1.0, 0.0)
    lower = jnp.where(causal, 1.0, 0.0)
    lane = lax.broadcasted_iota(jnp.int32, (1, L), 1)
    tn = (((0,), (0,)), ((), ()))
    nt = (((1,), (1,)), ((), ()))

    def to_col(r):
        return jnp.sum(eye * r, axis=-1, keepdims=True)

    def one_head(j, c, r0, cs, ns, ms):
        q_ref, k_ref, v_ref, og_ref = tok_refs[4 * j:4 * j + 4]
        q = q_ref[pl.ds(r0, L), :]
        k = k_ref[pl.ds(r0, L), :] * (HEAD_DIM ** -0.5)
        v = v_ref[pl.ds(r0, L), :]
        i_row = gi_ref[j, 0, pl.ds(c, 1), :] + bi_ref[j]
        f_pre = gf_ref[j, 0, pl.ds(c, 1), :] + bf_ref[j]
        f_row = jnp.minimum(f_pre, 0.0) - jnp.log(1.0 + jnp.exp(-jnp.abs(f_pre)))
        if n_valid < L:
            i_row = jnp.where(lane < n_valid, i_row, MASKED)
            f_row = jnp.where(lane < n_valid, f_row, 0.0)
        i_col = to_col(i_row)
        f_col = to_col(f_row)
        b_col = jnp.sum(lower * f_row, axis=-1, keepdims=True)
        b_row = jnp.sum(upper * f_col, axis=0, keepdims=True)
        a_intra = jnp.where(causal, b_col - b_row + i_row, MASKED)
        a_inter = b_col + ms
        m_t = jnp.maximum(a_inter, jnp.max(a_intra, axis=-1, keepdims=True))
        w_intra = jnp.exp(a_intra - m_t)
        w_inter = jnp.exp(a_inter - m_t)
        qb = q.astype(BF16)
        kb = k.astype(BF16)
        vb = v.astype(BF16)
        qk = lax.dot_general(qb, kb, nt, preferred_element_type=F32) * w_intra
        num = (w_inter * jnp.dot(qb, cs.astype(BF16), preferred_element_type=F32)
               + jnp.dot(qk.astype(BF16), vb, preferred_element_type=F32))
        qn = (w_inter * jnp.sum(q * ns, axis=-1, keepdims=True)
              + jnp.sum(qk, axis=-1, keepdims=True))
        h = num / jnp.maximum(jnp.abs(qn), jnp.exp(-m_t))
        m_new = m_t[L - 1:L, :]
        b_last = b_col[L - 1:L, :]
        decay = jnp.exp(b_last + ms - m_new)
        g_col = jnp.exp(b_last - b_col + i_col - m_new)
        kg = k * g_col
        c_new = decay * cs + lax.dot_general(kg.astype(BF16), vb, tn,
                                             preferred_element_type=F32)
        n_new = decay * ns + jnp.sum(kg, axis=0, keepdims=True)
        hg = jax.nn.sigmoid(og_ref[pl.ds(r0, L), :]) * h
        mu = jnp.mean(hg, axis=-1, keepdims=True)
        hc = hg - mu
        var = jnp.mean(hc * hc, axis=-1, keepdims=True)
        y_ref[pl.ds(r0, L), j * HEAD_DIM:(j + 1) * HEAD_DIM] = (
            hc * lax.rsqrt(var + LN_EPS) * ng_ref[j]).astype(y_ref.dtype)
        return c_new, n_new, m_new

    def body(c, carry):
        r0 = pl.multiple_of(c * L, L)
        out = []
        for j in range(hb):
            out.extend(one_head(j, c, r0, *carry[3 * j:3 * j + 3]))
        return tuple(out)

    init = []
    for j in range(hb):
        init.extend((c0_ref[0, j], n0_ref[0, j], m0_ref[0, j]))
    final = lax.fori_loop(0, n_chunks, body, tuple(init),
                          unroll=2 if n_chunks % 2 == 0 else 1)
    for j in range(hb):
        c_ref[0, j] = final[3 * j]
        n_ref[0, j] = final[3 * j + 1]
        m_ref[0, j] = final[3 * j + 2]


def _mlstm_group(proj, gates_t, b_i, b_f, norm_g, c0, n0, m0,
                 *, row0, batches, rows, n_valid, col0, name):
    heads = b_i.shape[0]
    hb = max(g for g in (4, 3, 2, 1) if heads % g == 0)
    chunk = MLSTM_CHUNK if n_valid % MLSTM_CHUNK == 0 else rows
    assert n_valid == rows or chunk == rows
    n_chunks = rows // chunk
    c0b = col0 // HEAD_DIM
    rb = row0 // rows
    g4 = gates_t[:, row0:row0 + batches * rows].reshape(2 * heads, batches, n_chunks, chunk)

    def tok(which, j):
        return pl.BlockSpec((rows, HEAD_DIM),
                            lambda b, h: (rb + b, c0b + which * heads + h * hb + j))

    gate = lambda off: pl.BlockSpec(
        (hb, 1, n_chunks, chunk), lambda b, h: (off // hb + h, b, 0, 0))
    scal = pl.BlockSpec((hb, 1, 1), lambda b, h: (h, 0, 0))
    cspec = pl.BlockSpec((1, hb, HEAD_DIM, HEAD_DIM), lambda b, h: (b, h, 0, 0))
    nspec = pl.BlockSpec((1, hb, 1, HEAD_DIM), lambda b, h: (b, h, 0, 0))
    mspec = pl.BlockSpec((1, hb, 1, 1), lambda b, h: (b, h, 0, 0))
    tok_specs = [tok(which, j) for j in range(hb) for which in range(4)]
    y, c, n, m = pl.pallas_call(
        functools.partial(_mlstm_kernel, chunk=chunk, n_chunks=n_chunks,
                          n_valid=min(n_valid, chunk), hb=hb),
        grid=(batches, heads // hb),
        in_specs=tok_specs + [gate(0), gate(heads), scal, scal,
                              pl.BlockSpec((hb, 1, HEAD_DIM), lambda b, h: (h, 0, 0)),
                              cspec, nspec, mspec],
        out_specs=[pl.BlockSpec((rows, hb * HEAD_DIM), lambda b, h: (b, h)), cspec, nspec, mspec],
        out_shape=[jax.ShapeDtypeStruct((batches * rows, heads * HEAD_DIM), BF16),
                   jax.ShapeDtypeStruct(c0.shape, F32),
                   jax.ShapeDtypeStruct(n0.shape[:2] + (1, HEAD_DIM), F32),
                   jax.ShapeDtypeStruct(m0.shape + (1, 1), F32)],
        compiler_params=_params("parallel", "parallel"),
        name=name,
    )(*([proj] * (4 * hb)), g4, g4,
      b_i.reshape(heads, 1, 1), b_f.reshape(heads, 1, 1), norm_g.reshape(heads, 1, HEAD_DIM),
      c0, n0.reshape(n0.shape[:2] + (1, HEAD_DIM)), m0.reshape(m0.shape + (1, 1)))
    return y, c, n.reshape(n0.shape), m.reshape(m0.shape)


def _topk_rows(vals, order, k, payload=None):
    out_v, out_p = [], []
    for _ in range(k):
        m = jnp.max(vals, axis=0, keepdims=True)
        first = jnp.min(jnp.where(vals == m, order, float(2 ** 20)), axis=0, keepdims=True)
        hit = order == first
        out_v.append(m)
        if payload is None:
            out_p.append(first)
        else:
            out_p.append(jnp.sum(jnp.where(hit, payload, 0.0), axis=0, keepdims=True))
        vals = jnp.where(hit, -jnp.inf, vals)
    return jnp.concatenate(out_v, axis=0), jnp.concatenate(out_p, axis=0)


def _candidate_pieces(k):
    split = 5
    valid = lambda a, b: (a + 1) * (b + 1) <= k
    pieces = []
    for b in range(min(split, k)):
        for a0 in range(0, k, 8):
            if any(valid(a, b) for a in range(a0, a0 + 8)):
                pieces.append(("col", a0, b))
    for a in range(k):
        for b0 in range(0, k, 8):
            if any(valid(a, b) and b >= split for b in range(b0, b0 + 8)):
                pieces.append(("row", a, b0))
    covered = []
    for kind, p, q in pieces:
        for r in range(8):
            a, b = (p + r, q) if kind == "col" else (p, q + r)
            if valid(a, b) and (kind == "col" or b >= split):
                covered.append((a, b))
    assert sorted(covered) == sorted((a, b) for a in range(k) for b in range(k) if valid(a, b))
    return pieces, split


def _pair_candidates(sv, si, n_keys, k):
    tok = sv[0].shape[1]
    pieces, split = _candidate_pieces(k)
    r8 = lax.broadcasted_iota(jnp.int32, (8, tok), 0)
    vals, order, eid = [], [], []
    for piece, (kind, p, q) in enumerate(pieces):
        if kind == "col":
            a_idx, b_idx = p + r8, jnp.full((8, tok), q, jnp.int32)
            v = sv[0][p:p + 8] + sv[1][q:q + 1]
            e = si[0][p:p + 8] * n_keys + si[1][q:q + 1]
            ok = (a_idx + 1) * (q + 1) <= k
        else:
            a_idx, b_idx = jnp.full((8, tok), p, jnp.int32), q + r8
            v = sv[0][p:p + 1] + sv[1][q:q + 8]
            e = si[0][p:p + 1] * n_keys + si[1][q:q + 8]
            ok = ((p + 1) * (b_idx + 1) <= k) & (b_idx >= split)
        vals.append(jnp.where(ok, v, -jnp.inf))
        unused = k * k + piece * 8 + r8
        order.append(jnp.where(ok, a_idx * k + b_idx, unused).astype(F32))
        eid.append(e)
    return (jnp.concatenate(vals, axis=0), jnp.concatenate(order, axis=0),
            jnp.concatenate(eid, axis=0))


def _route_kernel(q_ref, sk_ref, g_ref, it_ref, jt_ref, gt_ref, w_ref, *, heads, n_keys):
    tok = q_ref.shape[0]
    half = sk_ref.shape[-1]
    K = PEER_TOPK
    nt = (((1,), (1,)), ((), ()))
    key_id = lax.broadcasted_iota(jnp.int32, (n_keys, tok), 0).astype(F32)
    i_rows, j_rows, g_rows = [], [], []
    for h in range(heads):
        sv, si = [], []
        for p in range(2):
            c = (h * 2 + p) * half
            qh = q_ref[:, c:c + half].astype(BF16)
            st = lax.dot_general(sk_ref[h, p], qh, nt, preferred_element_type=F32)
            v, i = _topk_rows(st, key_id, K)
            sv.append(v)
            si.append(i)
        cand, flat, cid = _pair_candidates(sv, si, n_keys, K)
        cv, eid = _topk_rows(cand, flat, K, payload=cid)
        e = jnp.exp(cv - cv[0:1])
        gate = e / jnp.sum(e, axis=0, keepdims=True)
        ei = jnp.floor(eid * (1.0 / n_keys))
        i_rows.append(ei)
        j_rows.append(eid - ei * n_keys)
        g_rows.append(gate)
    it_ref[...] = jnp.concatenate(i_rows, axis=0).T
    jt_ref[...] = jnp.concatenate(j_rows, axis=0).T
    gt_ref[...] = jnp.concatenate(g_rows, axis=0).T

    picks = heads * K
    sub = lax.broadcasted_iota(jnp.int32, (n_keys, picks), 0).astype(F32)

    def per_token(n, _):
        irow = it_ref[pl.ds(n, 1), :]
        jrow = jt_ref[pl.ds(n, 1), :]
        grow = gt_ref[pl.ds(n, 1), :]
        a_t = jnp.where(sub == irow, 1.0, 0.0).astype(BF16)
        b_t = jnp.where(sub == jrow, grow, 0.0).astype(BF16)
        r0 = pl.multiple_of(n * n_keys, n_keys)
        w_ref[pl.ds(r0, n_keys), :] = lax.dot_general(a_t, b_t, nt, preferred_element_type=F32)
        return 0

    lax.fori_loop(0, tok, per_token, 0, unroll=ROUTE_UNROLL)
    for i in range(n_keys):
        g_ref[:, i * n_keys:(i + 1) * n_keys] = w_ref[pl.ds(i, tok, stride=n_keys), :].astype(g_ref.dtype)


def _route(qp, subkeys_bf, *, name):
    n = qp.shape[0]
    heads, _, n_keys, half = subkeys_bf.shape
    tok = LANES
    picks = heads * PEER_TOPK
    return pl.pallas_call(
        functools.partial(_route_kernel, heads=heads, n_keys=n_keys),
        grid=(n // tok,),
        in_specs=[pl.BlockSpec((tok, qp.shape[1]), lambda i: (i, 0)),
                  pl.BlockSpec(subkeys_bf.shape, lambda i: (0, 0, 0, 0))],
        out_specs=pl.BlockSpec((tok, n_keys * n_keys), lambda i: (i, 0)),
        out_shape=jax.ShapeDtypeStruct((n, n_keys * n_keys), BF16),
        scratch_shapes=[pltpu.VMEM((tok, picks), F32)] * 3
        + [pltpu.VMEM((tok * n_keys, n_keys), F32)],
        compiler_params=_params("parallel"),
        name=name,
    )(qp, subkeys_bf)


def _peer_act_kernel(x_ref, u_ref, g_ref, p_ref):
    s = lax.dot_general(x_ref[...], u_ref[...].astype(BF16), (((1,), (1,)), ((), ())),
                        preferred_element_type=F32)
    act = 0.5 * s * (1.0 + lax.erf(s * (1.0 / math.sqrt(2.0))))
    p_ref[...] = (act * g_ref[...].astype(F32)).astype(p_ref.dtype)


def _peer_act(xb, u, g, *, layer, tn, te, name):
    n, d = xb.shape
    n_exp = u.shape[1]
    return pl.pallas_call(
        _peer_act_kernel,
        grid=(n // tn, n_exp // te),
        in_specs=[pl.BlockSpec((tn, d), lambda i, e: (i, 0), pipeline_mode=pl.Buffered(1)),
                  pl.BlockSpec((None, te, d), lambda i, e: (layer, e, 0)),
                  pl.BlockSpec((tn, te), lambda i, e: (i, e))],
        out_specs=pl.BlockSpec((tn, te), lambda i, e: (i, e)),
        out_shape=jax.ShapeDtypeStruct((n, n_exp), BF16),
        compiler_params=_params("parallel", "parallel"),
        name=name,
    )(xb, u, g)


def _kv_heads_kernel(*refs, depth):
    ins, outs = refs[:-2], refs[-2:]
    for l in range(depth):
        for t, dst in enumerate(outs):
            dst[l] = ins[l * 2 + t][...]


def _kv_prompt_heads(projs, *, batches, t_len, heads, col_k, col_v, name):
    depth = len(projs)
    specs, args = [], []
    for proj in projs:
        for col in (col_k, col_v):
            specs.append(pl.BlockSpec((t_len, HEAD_DIM),
                                      lambda b, h, cb=col // HEAD_DIM: (b, cb + h)))
            args.append(proj)
    dst = pl.BlockSpec((depth, None, None, t_len, HEAD_DIM), lambda b, h: (0, b, h, 0, 0))
    shape = jax.ShapeDtypeStruct((depth, batches, heads, t_len, HEAD_DIM), F32)
    return pl.pallas_call(
        functools.partial(_kv_heads_kernel, depth=depth),
        grid=(batches, heads),
        in_specs=specs,
        out_specs=[dst, dst],
        out_shape=[shape, shape],
        compiler_params=_params("parallel", "parallel"),
        name=name,
    )(*args)


def _layer(x, xb, st, w, big, *, layer, dims, alpha):
    bp, tp, bs, ts = dims
    tag = str(layer)
    n, d = x.shape
    rows_p = bp * tp
    conv_ch = w["conv_w"].shape[-1]
    a_heads = big["cache_k"].shape[2]
    m_heads = w["b_i"].shape[0]
    attn_w = a_heads * HEAD_DIM
    mlstm_w = m_heads * HEAD_DIM
    off_att = 2 * conv_ch
    off_mls = off_att + 3 * attn_w
    off_gate = off_mls + 4 * mlstm_w

    tn_big = _pick_tile(n, 2080, SUBLANES_BF16)
    tn_acc = _pick_tile(n, 1664, SUBLANES_BF16)
    proj = _matmul(xb, big["w_in_t"], layer=layer, width=off_gate, tn=tn_big, transposed=True,
                   tm=_pick_tile(off_gate, 512, LANES), name=f"proj{tag}")
    gates = _matmul(xb, big["w_gate"], layer=layer, width=LANES, tn=tn_big, tm=LANES,
                    name=f"gates{tag}")
    gates_t = gates[:, :2 * m_heads].T

    zeros = lambda *s: jnp.zeros(s, F32)
    conv_args = (w["conv_w"], w["conv_b"], w["conv_ln_g"], w["conv_ln_b"])
    ya_p, conv_p = _conv_group(proj, zeros(bp, w["conv_w"].shape[0] - 1, conv_ch), *conv_args,
                               row0=0, batches=bp, rows=tp, n_valid=tp, name=f"conv_p{tag}")
    ya_s, conv_s = _conv_group(proj, st["conv"], *conv_args, row0=rows_p, batches=bs,
                               rows=SAMPLE_ROWS, n_valid=ts, name=f"conv_s{tag}")

    yb_p = _attn_prompt(proj, batches=bp, t_len=tp, heads=a_heads, col0=off_att,
                        name=f"attn_p{tag}")
    yb_s = _attn_sample(proj, big["cache_k"], big["cache_v"], layer=layer, row0=rows_p,
                        batches=bs, heads=a_heads, col0=off_att, n_valid=ts, name=f"attn_s{tag}")

    ml_args = (w["b_i"], w["b_f"], w["norm_g"])
    yc_p, c_p, n_p, m_p = _mlstm_group(
        proj, gates_t, *ml_args, zeros(bp, m_heads, HEAD_DIM, HEAD_DIM),
        zeros(bp, m_heads, HEAD_DIM), zeros(bp, m_heads),
        row0=0, batches=bp, rows=tp, n_valid=tp, col0=off_mls, name=f"mlstm_p{tag}")
    yc_s, c_s, n_s, m_s = _mlstm_group(
        proj, gates_t, *ml_args, st["c"], st["n"], st["m"],
        row0=rows_p, batches=bs, rows=SAMPLE_ROWS, n_valid=ts, col0=off_mls,
        name=f"mlstm_s{tag}")

    rows_s = bs * SAMPLE_ROWS
    ymix = jnp.concatenate([
        jnp.concatenate([ya_p, yb_p, yc_p], axis=1),
        jnp.concatenate([ya_s, yb_s, yc_s], axis=1),
        jnp.zeros((n - rows_p - rows_s, d), BF16)], axis=0)
    mix = _matmul(ymix, big["w_out"], layer=layer, width=d, tn=tn_big,
                  tm=_pick_tile(d, 512, LANES), name=f"outproj{tag}")
    tn_ln = _pick_tile(n, 256, SUBLANES_BF16)
    x1, x1b = _res_ln(x, mix, w["ln1_g"], w["ln1_b"], alpha=alpha, tn=tn_ln, name=f"ln1{tag}")

    q_dim = big["peer_wq"].shape[2]
    qp = _matmul(x1b, big["peer_wq"], layer=layer, width=q_dim, tn=tn_big,
                 tm=_pick_tile(q_dim, 512, LANES), name=f"peer_q{tag}")
    g = _route(qp, w["peer_subkeys"], name=f"route{tag}")
    n_exp = big["peer_u"].shape[1]
    p = _peer_act(x1b, big["peer_u"], g, layer=layer, tn=tn_big,
                  te=_pick_tile(n_exp, 512, LANES), name=f"peer_act{tag}")
    po = _matmul_ktiled(p, big["peer_v"], layer=layer, tn=tn_acc, tm=_pick_tile(d, 1024, LANES),
                        tk=_pick_tile(n_exp, 2048, LANES), name=f"peer_out{tag}")
    x2, x2b = _res_ln(x1, po, w["ln2_g"], w["ln2_b"], alpha=alpha, tn=tn_ln, name=f"ln2{tag}")

    kv = lambda rows, b, t, which: rows[:, off_att + which * attn_w:off_att + (which + 1) * attn_w
                                        ].reshape(b, t, a_heads, HEAD_DIM)
    sr = proj[rows_p:rows_p + rows_s]
    outs = dict(
        proj=proj, kv_cols=(off_att + attn_w, off_att + 2 * attn_w),
        k_s=kv(sr, bs, SAMPLE_ROWS, 1)[:, :ts], v_s=kv(sr, bs, SAMPLE_ROWS, 2)[:, :ts],
        conv_p=conv_p, conv_s=conv_s, c_p=c_p, c_s=c_s, n_p=n_p, n_s=n_s, m_p=m_p, m_s=m_s)
    return x2, x2b, outs


def kernel(x_prompt, x_sample, cache_attn_k, cache_attn_v, state_conv, state_mlstm_c, state_mlstm_n, state_mlstm_m, w_in, conv_w, conv_b, conv_ln_g, conv_ln_b, mlstm_b_i, mlstm_b_f, mlstm_norm_g, w_out, ln1_g, ln1_b, peer_wq, peer_subkeys, peer_u, peer_v, ln2_g, ln2_b):
    bp, tp, d = x_prompt.shape
    bs, ts, _ = x_sample.shape
    depth = w_in.shape[0]
    assert ts <= SAMPLE_ROWS and tp % SAMPLE_ROWS == 0
    rows_p = bp * tp
    rows_s = bs * SAMPLE_ROWS
    n = -(-(rows_p + rows_s) // LANES) * LANES
    xs = jnp.pad(x_sample, ((0, 0), (0, SAMPLE_ROWS - ts), (0, 0))).reshape(rows_s, d)
    x = jnp.concatenate([x_prompt.reshape(rows_p, d), xs,
                         jnp.zeros((n - rows_p - rows_s, d), x_prompt.dtype)], axis=0)
    xb = x.astype(BF16)
    alpha = (2 * depth) ** 0.25

    n_gate = 2 * mlstm_b_i.shape[1]
    w_gate = w_in[:, :, w_in.shape[2] - n_gate:].astype(BF16)
    big = dict(
        w_in_t=jnp.transpose(w_in, (0, 2, 1)),
        w_gate=jnp.pad(w_gate, ((0, 0), (0, 0), (0, LANES - n_gate))),
        w_out=w_out, peer_wq=peer_wq, peer_u=peer_u, peer_v=peer_v,
        cache_k=jnp.transpose(cache_attn_k, (0, 1, 3, 2, 4)),
        cache_v=jnp.transpose(cache_attn_v, (0, 1, 3, 2, 4)))
    subkeys_b = peer_subkeys.astype(BF16)

    per_layer = []
    for l in range(depth):
        st = dict(conv=state_conv[l], c=state_mlstm_c[l], n=state_mlstm_n[l], m=state_mlstm_m[l])
        w = dict(conv_w=conv_w[l], conv_b=conv_b[l], conv_ln_g=conv_ln_g[l],
                 conv_ln_b=conv_ln_b[l], b_i=mlstm_b_i[l], b_f=mlstm_b_f[l],
                 norm_g=mlstm_norm_g[l], ln1_g=ln1_g[l], ln1_b=ln1_b[l],
                 peer_subkeys=subkeys_b[l], ln2_g=ln2_g[l], ln2_b=ln2_b[l])
        x, xb, outs = _layer(x, xb, st, w, big, layer=l, dims=(bp, tp, bs, ts), alpha=alpha)
        per_layer.append(outs)

    a_heads = cache_attn_k.shape[3]
    col_k, col_v = per_layer[0]["kv_cols"]
    k_all, v_all = _kv_prompt_heads([o["proj"] for o in per_layer], batches=bp, t_len=tp,
                                    heads=a_heads, col_k=col_k, col_v=col_v, name="kv_prompt")
    win = min(max(wd for wd, _ in DILATED_PATTERNS), tp)
    k_prompt = jnp.transpose(k_all, (0, 1, 3, 2, 4))[:, :, tp - win:]
    v_prompt = jnp.transpose(v_all, (0, 1, 3, 2, 4))[:, :, tp - win:]

    stack = lambda key: jnp.stack([o[key] for o in per_layer])
    y_prompt = x[:rows_p].reshape(bp, tp, d)
    y_sample = x[rows_p:rows_p + rows_s].reshape(bs, SAMPLE_ROWS, d)[:, :ts]
    return (y_prompt, y_sample, k_prompt, v_prompt, stack("k_s"), stack("v_s"),
            stack("conv_p"), stack("conv_s"), stack("c_p"), stack("c_s"),
            stack("n_p"), stack("n_s"), stack("m_p"), stack("m_s"))
```

```python
import functools
import math

import jax
import jax.numpy as jnp
from jax import lax
from jax.experimental import pallas as pl
from jax.experimental.pallas import tpu as pltpu

F32 = jnp.float32
BF16 = jnp.bfloat16

LANES = 128
SUBLANES_BF16 = 16
VMEM_LIMIT_BYTES = 56 * 1024 * 1024

HEAD_DIM = 128
DILATED_PATTERNS = ((128, 1), (512, 4), (2048, 16))
MLSTM_CHUNK = 64
PEER_TOPK = 16
LN_EPS = 1e-5
SAMPLE_ROWS = 16
CONV_HALO = 32
MASKED = -1e30
ROUTE_UNROLL = 8
MLSTM_HEAD_GROUPS = (4, 3, 2, 1)
MLSTM_BLOCK_CHUNKS = 32


def _params(*sem):
    return pltpu.CompilerParams(dimension_semantics=sem,
                                vmem_limit_bytes=VMEM_LIMIT_BYTES)


def _pick_tile(n, target, align):
    best = None
    for t in range(align, min(n, target) + 1, align):
        if n % t == 0:
            best = t
    assert best is not None, (n, target, align)
    return best


def _mm_kernel(a_ref, w_ref, o_ref, *, transposed):
    dims = (((1,), (1,)), ((), ())) if transposed else (((1,), (0,)), ((), ()))
    o_ref[...] = lax.dot_general(a_ref[...], w_ref[...].astype(BF16), dims,
                                 preferred_element_type=F32)


def _matmul(a, w, *, layer, width, tn, tm, name, transposed=False):
    n, k = a.shape
    if transposed:
        w_spec = pl.BlockSpec((None, tm, k), lambda i, j: (layer, j, 0))
    else:
        w_spec = pl.BlockSpec((None, k, tm), lambda i, j: (layer, 0, j))
    return pl.pallas_call(
        functools.partial(_mm_kernel, transposed=transposed),
        grid=(n // tn, width // tm),
        in_specs=[pl.BlockSpec((tn, k), lambda i, j: (i, 0), pipeline_mode=pl.Buffered(1)),
                  w_spec],
        out_specs=pl.BlockSpec((tn, tm), lambda i, j: (i, j)),
        out_shape=jax.ShapeDtypeStruct((n, width), F32),
        compiler_params=_params("parallel", "parallel"),
        name=name,
    )(a, w)


def _mm_acc_kernel(a_ref, w_ref, o_ref):
    part = jnp.dot(a_ref[...], w_ref[...].astype(BF16), preferred_element_type=F32)

    @pl.when(pl.program_id(2) == 0)
    def _():
        o_ref[...] = part

    @pl.when(pl.program_id(2) > 0)
    def _():
        o_ref[...] += part


def _matmul_ktiled(a, w, *, layer, tn, tm, tk, name):
    n, k = a.shape
    m = w.shape[2]
    return pl.pallas_call(
        _mm_acc_kernel,
        grid=(n // tn, m // tm, k // tk),
        in_specs=[pl.BlockSpec((tn, tk), lambda i, j, q: (i, q)),
                  pl.BlockSpec((None, tk, tm), lambda i, j, q: (layer, q, j))],
        out_specs=pl.BlockSpec((tn, tm), lambda i, j, q: (i, j)),
        out_shape=jax.ShapeDtypeStruct((n, m), F32),
        compiler_params=_params("parallel", "parallel", "arbitrary"),
        name=name,
    )(a, w)


def _mm_parts_kernel(*refs, parts):
    a_refs, w_ref, o_ref = refs[:parts], refs[parts], refs[parts + 1]
    acc, lo = None, 0
    for a_ref in a_refs:
        hi = lo + a_ref.shape[1]
        part = jnp.dot(a_ref[...], w_ref[lo:hi, :].astype(BF16), preferred_element_type=F32)
        acc = part if acc is None else acc + part
        lo = hi
    o_ref[...] = acc


def _matmul_parts(a_parts, w, *, layer, tn, tm, name):
    n = a_parts[0].shape[0]
    k, m = w.shape[1], w.shape[2]
    assert sum(a.shape[1] for a in a_parts) == k
    a_specs = [pl.BlockSpec((tn, a.shape[1]), lambda i, j: (i, 0), pipeline_mode=pl.Buffered(1))
               for a in a_parts]
    return pl.pallas_call(
        functools.partial(_mm_parts_kernel, parts=len(a_parts)),
        grid=(n // tn, m // tm),
        in_specs=a_specs + [pl.BlockSpec((None, k, tm), lambda i, j: (layer, 0, j))],
        out_specs=pl.BlockSpec((tn, tm), lambda i, j: (i, j)),
        out_shape=jax.ShapeDtypeStruct((n, m), F32),
        compiler_params=_params("parallel", "parallel"),
        name=name,
    )(*a_parts, w)


def _fill_tail_kernel(*refs, count):
    for src, out in zip(refs[:count], refs[2 * count:]):
        out[...] = src[...]


def _fill_tail(dsts, srcs, *, row0, name):
    tail = srcs[0].shape[0]
    assert row0 % tail == 0 and all(d.shape[0] == row0 + tail for d in dsts)
    count = len(dsts)
    return pl.pallas_call(
        functools.partial(_fill_tail_kernel, count=count),
        grid=(1,),
        in_specs=[pl.BlockSpec(s.shape, lambda i: (0, 0)) for s in srcs]
        + [pl.BlockSpec(memory_space=pl.ANY)] * count,
        out_specs=[pl.BlockSpec((tail, d.shape[1]), lambda i: (row0 // tail, 0)) for d in dsts],
        out_shape=[jax.ShapeDtypeStruct(d.shape, d.dtype) for d in dsts],
        input_output_aliases={count + i: i for i in range(count)},
        compiler_params=_params("arbitrary"),
        name=name,
    )(*srcs, *dsts)


def _res_ln_kernel(x_ref, y_ref, g_ref, b_ref, o_ref, ob_ref, *, alpha):
    z = alpha * x_ref[...] + y_ref[...]
    mu = jnp.mean(z, axis=-1, keepdims=True)
    zc = z - mu
    var = jnp.mean(zc * zc, axis=-1, keepdims=True)
    out = zc * lax.rsqrt(var + LN_EPS) * g_ref[...] + b_ref[...]
    o_ref[...] = out
    ob_ref[...] = out.astype(BF16)


def _res_ln(x, y, g, b, *, alpha, tn, name):
    n, d = x.shape
    row = pl.BlockSpec((tn, d), lambda i: (i, 0))
    vec = pl.BlockSpec((1, d), lambda i: (0, 0))
    return pl.pallas_call(
        functools.partial(_res_ln_kernel, alpha=alpha),
        grid=(n // tn,),
        in_specs=[row, row, vec, vec],
        out_specs=[row, row],
        out_shape=[jax.ShapeDtypeStruct((n, d), F32),
                   jax.ShapeDtypeStruct((n, d), BF16)],
        compiler_params=_params("parallel"),
        name=name,
    )(x, y, g.reshape(1, d), b.reshape(1, d))


def _conv_kernel(a_ref, gate_ref, prev_ref, cw_ref, cb_ref, lg_ref, lb_ref,
                 y_ref, new_ref, ext_ref, *, tt, n_valid_last, width):
    t = pl.program_id(1)
    keep = width - 1

    @pl.when(t == 0)
    def _():
        ext_ref[0:CONV_HALO, :] = jnp.zeros((CONV_HALO, ext_ref.shape[1]), F32)
        ext_ref[CONV_HALO - keep:CONV_HALO, :] = prev_ref[0]

    ext_ref[CONV_HALO:CONV_HALO + tt, :] = a_ref[...] * jax.nn.sigmoid(gate_ref[...])
    acc = jnp.zeros((tt, ext_ref.shape[1]), F32)
    for w in range(width):
        lo = CONV_HALO - keep + w
        acc = acc + ext_ref[lo:lo + tt, :] * cw_ref[w:w + 1, :]
    y = acc + cb_ref[...]
    mu = jnp.mean(y, axis=-1, keepdims=True)
    yc = y - mu
    var = jnp.mean(yc * yc, axis=-1, keepdims=True)
    y = yc * lax.rsqrt(var + LN_EPS) * lg_ref[...] + lb_ref[...]
    y_ref[...] = (y * jax.nn.sigmoid(y)).astype(y_ref.dtype)

    @pl.when(t == pl.num_programs(1) - 1)
    def _():
        hi = CONV_HALO + n_valid_last
        new_ref[0] = ext_ref[hi - keep:hi, :]

    ext_ref[0:CONV_HALO, :] = ext_ref[tt:tt + CONV_HALO, :]


def _conv_group(proj, prev, cw, cb, lg, lb, *, row0, batches, rows, n_valid, name, out_rows=None):
    width, ch = cw.shape
    tt = _pick_tile(rows, 256, 8)
    nt = rows // tt
    blk0 = row0 // tt
    assert row0 % tt == 0 and n_valid > rows - tt
    n_valid_last = n_valid - (nt - 1) * tt
    rowspec = lambda col: pl.BlockSpec((tt, ch), lambda b, t: (blk0 + b * nt + t, col))
    vec = pl.BlockSpec((1, ch), lambda b, t: (0, 0))
    state = pl.BlockSpec((1, width - 1, ch), lambda b, t: (b, 0, 0))
    return pl.pallas_call(
        functools.partial(_conv_kernel, tt=tt, n_valid_last=n_valid_last, width=width),
        grid=(batches, nt),
        in_specs=[rowspec(0), rowspec(1), state,
                  pl.BlockSpec((width, ch), lambda b, t: (0, 0)), vec, vec, vec],
        out_specs=[pl.BlockSpec((tt, ch), lambda b, t: (b * nt + t, 0)), state],
        out_shape=[jax.ShapeDtypeStruct((out_rows or batches * rows, ch), BF16),
                   jax.ShapeDtypeStruct((batches, width - 1, ch), F32)],
        scratch_shapes=[pltpu.VMEM((CONV_HALO + tt + CONV_HALO, ch), F32)],
        compiler_params=_params("parallel", "arbitrary"),
        name=name,
    )(proj, proj, prev, cw, cb.reshape(1, ch), lg.reshape(1, ch), lb.reshape(1, ch))


def _pattern_count(delta):
    cnt = jnp.zeros(delta.shape, F32)
    for window, dil in DILATED_PATTERNS:
        hit = (delta <= window) & ((delta & (dil - 1)) == 0)
        cnt = cnt + jnp.where(hit, 1.0, 0.0)
    return jnp.where(delta >= 0, cnt, 0.0)


def _attn_prompt_kernel(q_ref, k_ref, v_ref, o_ref, *, qb, scale):
    t_len = q_ref.shape[0]
    kb = k_ref[...].astype(BF16)
    vb = v_ref[...].astype(BF16)
    for i in range(t_len // qb):
        hi = (i + 1) * qb
        q = q_ref[i * qb:hi, :].astype(BF16)
        s = lax.dot_general(q, kb[0:hi], (((1,), (1,)), ((), ())),
                            preferred_element_type=F32) * scale
        qpos = i * qb + lax.broadcasted_iota(jnp.int32, (qb, hi), 0)
        kpos = lax.broadcasted_iota(jnp.int32, (qb, hi), 1)
        cnt = _pattern_count(qpos - kpos)
        s = jnp.where(cnt > 0, s, MASKED)
        mx = jnp.max(s, axis=-1, keepdims=True)
        e = jnp.exp(s - mx) * cnt
        den = jnp.sum(e, axis=-1, keepdims=True)
        out = jnp.dot(e.astype(BF16), vb[0:hi], preferred_element_type=F32)
        o_ref[i * qb:hi, :] = (out / den).astype(o_ref.dtype)


def _attn_prompt(proj, *, batches, t_len, heads, col0, name, out_rows=None):
    c0 = col0 // HEAD_DIM
    qb = _pick_tile(t_len, 256, 8)
    spec = lambda which: pl.BlockSpec(
        (t_len, HEAD_DIM), lambda b, h: (b, c0 + which * heads + h))
    return pl.pallas_call(
        functools.partial(_attn_prompt_kernel, qb=qb, scale=HEAD_DIM ** -0.5),
        grid=(batches, heads),
        in_specs=[spec(0), spec(1), spec(2)],
        out_specs=pl.BlockSpec((t_len, HEAD_DIM), lambda b, h: (b, h)),
        out_shape=jax.ShapeDtypeStruct((out_rows or batches * t_len, heads * HEAD_DIM), BF16),
        compiler_params=_params("parallel", "parallel"),
        name=name,
    )(proj, proj, proj)


def _attn_sample_kernel(q_ref, k_ref, v_ref, kc_ref, vc_ref, o_ref, *, n_valid, scale):
    rows = q_ref.shape[0]
    n_prev = kc_ref.shape[0]
    q = q_ref[...].astype(BF16)
    nt = (((1,), (1,)), ((), ()))
    s_c = lax.dot_general(q, kc_ref[...].astype(BF16), nt, preferred_element_type=F32) * scale
    s_n = lax.dot_general(q, k_ref[...].astype(BF16), nt, preferred_element_type=F32) * scale
    qpos_c = n_prev + lax.broadcasted_iota(jnp.int32, (rows, n_prev), 0)
    cnt_c = _pattern_count(qpos_c - lax.broadcasted_iota(jnp.int32, (rows, n_prev), 1))
    tq = lax.broadcasted_iota(jnp.int32, (rows, rows), 0)
    tk = lax.broadcasted_iota(jnp.int32, (rows, rows), 1)
    cnt_n = jnp.where(tk < n_valid, _pattern_count(tq - tk), 0.0)
    s_c = jnp.where(cnt_c > 0, s_c, MASKED)
    s_n = jnp.where(cnt_n > 0, s_n, MASKED)
    mx = jnp.maximum(jnp.max(s_c, axis=-1, keepdims=True),
                     jnp.max(s_n, axis=-1, keepdims=True))
    e_c = jnp.exp(s_c - mx) * cnt_c
    e_n = jnp.exp(s_n - mx) * cnt_n
    den = jnp.sum(e_c, axis=-1, keepdims=True) + jnp.sum(e_n, axis=-1, keepdims=True)
    out = (jnp.dot(e_c.astype(BF16), vc_ref[...].astype(BF16), preferred_element_type=F32)
           + jnp.dot(e_n.astype(BF16), v_ref[...].astype(BF16), preferred_element_type=F32))
    o_ref[...] = (out / den).astype(o_ref.dtype)


def _attn_sample(proj, cache_k, cache_v, *, layer, row0, batches, heads, col0, n_valid, name):
    c0 = col0 // HEAD_DIM
    r0 = row0 // SAMPLE_ROWS
    n_prev = cache_k.shape[3]
    new = lambda which: pl.BlockSpec(
        (SAMPLE_ROWS, HEAD_DIM), lambda b, h: (r0 + b, c0 + which * heads + h))
    cache = pl.BlockSpec((None, None, None, n_prev, HEAD_DIM), lambda b, h: (layer, b, h, 0, 0))
    return pl.pallas_call(
        functools.partial(_attn_sample_kernel, n_valid=n_valid, scale=HEAD_DIM ** -0.5),
        grid=(batches, heads),
        in_specs=[new(0), new(1), new(2), cache, cache],
        out_specs=pl.BlockSpec((SAMPLE_ROWS, HEAD_DIM), lambda b, h: (b, h)),
        out_shape=jax.ShapeDtypeStruct((batches * SAMPLE_ROWS, heads * HEAD_DIM), BF16),
        compiler_params=_params("parallel", "parallel"),
        name=name,
    )(proj, proj, proj, cache_k, cache_v)


def _mlstm_kernel(*refs, chunk, n_chunks, n_valid, hb, unroll):
    tok_refs = refs[:4 * hb]
    (gi_ref, gf_ref, bi_ref, bf_ref, ng_ref, c0_ref, n0_ref, m0_ref,
     y_ref, c_ref, n_ref, m_ref) = refs[4 * hb:]
    L = chunk
    row = lax.broadcasted_iota(jnp.int32, (L, L), 0)
    col = lax.broadcasted_iota(jnp.int32, (L, L), 1)
    causal = col <= row
    eye = jnp.where(row == col, 1.0, 0.0)
    upper = jnp.where(row <= col, 1.0, 0.0)
    lower = jnp.where(causal, 1.0, 0.0)
    lane = lax.broadcasted_iota(jnp.int32, (1, L), 1)
    tn = (((0,), (0,)), ((), ()))
    nt = (((1,), (1,)), ((), ()))

    def to_col(r):
        return jnp.sum(eye * r, axis=-1, keepdims=True)

    def one_head(j, c, r0, cs, ns, ms):
        q_ref, k_ref, v_ref, og_ref = tok_refs[4 * j:4 * j + 4]
        q = q_ref[pl.ds(r0, L), :]
        k = k_ref[pl.ds(r0, L), :] * (HEAD_DIM ** -0.5)
        v = v_ref[pl.ds(r0, L), :]
        i_row = gi_ref[j, 0, pl.ds(c, 1), :] + bi_ref[j]
        f_pre = gf_ref[j, 0, pl.ds(c, 1), :] + bf_ref[j]
        f_row = jnp.minimum(f_pre, 0.0) - jnp.log(1.0 + jnp.exp(-jnp.abs(f_pre)))
        if n_valid < L:
            i_row = jnp.where(lane < n_valid, i_row, MASKED)
            f_row = jnp.where(lane < n_valid, f_row, 0.0)
        i_col = to_col(i_row)
        f_col = to_col(f_row)
        b_col = jnp.sum(lower * f_row, axis=-1, keepdims=True)
        b_row = jnp.sum(upper * f_col, axis=0, keepdims=True)
        a_intra = jnp.where(causal, b_col - b_row + i_row, MASKED)
        a_inter = b_col + ms
        m_t = jnp.maximum(a_inter, jnp.max(a_intra, axis=-1, keepdims=True))
        w_intra = jnp.exp(a_intra - m_t)
        w_inter = jnp.exp(a_inter - m_t)
        qb = q.astype(BF16)
        kb = k.astype(BF16)
        vb = v.astype(BF16)
        qk = lax.dot_general(qb, kb, nt, preferred_element_type=F32) * w_intra
        num = (w_inter * jnp.dot(qb, cs.astype(BF16), preferred_element_type=F32)
               + jnp.dot(qk.astype(BF16), vb, preferred_element_type=F32))
        qn = (w_inter * jnp.sum(q * ns, axis=-1, keepdims=True)
              + jnp.sum(qk, axis=-1, keepdims=True))
        h = num / jnp.maximum(jnp.abs(qn), jnp.exp(-m_t))
        m_new = m_t[L - 1:L, :]
        b_last = b_col[L - 1:L, :]
        decay = jnp.exp(b_last + ms - m_new)
        g_col = jnp.exp(b_last - b_col + i_col - m_new)
        kg = k * g_col
        c_new = decay * cs + lax.dot_general(kg.astype(BF16), vb, tn,
                                             preferred_element_type=F32)
        n_new = decay * ns + jnp.sum(kg, axis=0, keepdims=True)
        hg = jax.nn.sigmoid(og_ref[pl.ds(r0, L), :]) * h
        mu = jnp.mean(hg, axis=-1, keepdims=True)
        hc = hg - mu
        var = jnp.mean(hc * hc, axis=-1, keepdims=True)
        y_ref[pl.ds(r0, L), j * HEAD_DIM:(j + 1) * HEAD_DIM] = (
            hc * lax.rsqrt(var + LN_EPS) * ng_ref[j]).astype(y_ref.dtype)
        return c_new, n_new, m_new

    def body(c, carry):
        r0 = pl.multiple_of(c * L, L)
        out = []
        for j in range(hb):
            out.extend(one_head(j, c, r0, *carry[3 * j:3 * j + 3]))
        return tuple(out)

    @pl.when(pl.program_id(2) == 0)
    def _():
        c_ref[...] = c0_ref[...]
        n_ref[...] = n0_ref[...]
        m_ref[...] = m0_ref[...]

    init = []
    for j in range(hb):
        init.extend((c_ref[0, j], n_ref[0, j], m_ref[0, j]))
    final = lax.fori_loop(0, n_chunks, body, tuple(init), unroll=unroll)
    for j in range(hb):
        c_ref[0, j] = final[3 * j]
        n_ref[0, j] = final[3 * j + 1]
        m_ref[0, j] = final[3 * j + 2]


def _mlstm_group(proj, gates_t, b_i, b_f, norm_g, c0, n0, m0,
                 *, row0, batches, rows, n_valid, col0, name, out_rows=None):
    heads = b_i.shape[0]
    hb = max(g for g in MLSTM_HEAD_GROUPS if heads % g == 0)
    chunk = MLSTM_CHUNK if n_valid % MLSTM_CHUNK == 0 else rows
    assert n_valid == rows or chunk == rows
    n_chunks = rows // chunk
    blk_chunks = _pick_tile(n_chunks, MLSTM_BLOCK_CHUNKS, 8) if n_chunks % 8 == 0 else n_chunks
    tb = blk_chunks * chunk
    nt = rows // tb
    c0b = col0 // HEAD_DIM
    rb = row0 // tb
    g4 = gates_t[:, row0:row0 + batches * rows].reshape(2 * heads, batches, n_chunks, chunk)

    def tok(which, j):
        return pl.BlockSpec((tb, HEAD_DIM),
                            lambda b, h, t: (rb + b * nt + t, c0b + which * heads + h * hb + j))

    gate = lambda off: pl.BlockSpec(
        (hb, 1, blk_chunks, chunk), lambda b, h, t: (off // hb + h, b, t, 0))
    scal = pl.BlockSpec((hb, 1, 1), lambda b, h, t: (h, 0, 0))
    cspec = pl.BlockSpec((1, hb, HEAD_DIM, HEAD_DIM), lambda b, h, t: (b, h, 0, 0))
    nspec = pl.BlockSpec((1, hb, 1, HEAD_DIM), lambda b, h, t: (b, h, 0, 0))
    mspec = pl.BlockSpec((1, hb, 1, 1), lambda b, h, t: (b, h, 0, 0))
    tok_specs = [tok(which, j) for j in range(hb) for which in range(4)]
    y, c, n, m = pl.pallas_call(
        functools.partial(_mlstm_kernel, chunk=chunk, n_chunks=blk_chunks,
                          n_valid=min(n_valid, chunk), hb=hb,
                          unroll=2 if blk_chunks % 2 == 0 else 1),
        grid=(batches, heads // hb, nt),
        in_specs=tok_specs + [gate(0), gate(heads), scal, scal,
                              pl.BlockSpec((hb, 1, HEAD_DIM), lambda b, h, t: (h, 0, 0)),
                              cspec, nspec, mspec],
        out_specs=[pl.BlockSpec((tb, hb * HEAD_DIM), lambda b, h, t: (b * nt + t, h)),
                   cspec, nspec, mspec],
        out_shape=[jax.ShapeDtypeStruct((out_rows or batches * rows, heads * HEAD_DIM), BF16),
                   jax.ShapeDtypeStruct(c0.shape, F32),
                   jax.ShapeDtypeStruct(n0.shape[:2] + (1, HEAD_DIM), F32),
                   jax.ShapeDtypeStruct(m0.shape + (1, 1), F32)],
        compiler_params=_params("parallel", "parallel", "arbitrary"),
        name=name,
    )(*([proj] * (4 * hb)), g4, g4,
      b_i.reshape(heads, 1, 1), b_f.reshape(heads, 1, 1), norm_g.reshape(heads, 1, HEAD_DIM),
      c0, n0.reshape(n0.shape[:2] + (1, HEAD_DIM)), m0.reshape(m0.shape + (1, 1)))
    return y, c, n.reshape(n0.shape), m.reshape(m0.shape)


def _topk_rows(vals, order, k, payload=None):
    out_v, out_p = [], []
    for _ in range(k):
        m = jnp.max(vals, axis=0, keepdims=True)
        first = jnp.min(jnp.where(vals == m, order, float(2 ** 20)), axis=0, keepdims=True)
        hit = order == first
        out_v.append(m)
        if payload is None:
            out_p.append(first)
        else:
            out_p.append(jnp.sum(jnp.where(hit, payload, 0.0), axis=0, keepdims=True))
        vals = jnp.where(hit, -jnp.inf, vals)
    return jnp.concatenate(out_v, axis=0), jnp.concatenate(out_p, axis=0)


def _candidate_pieces(k):
    split = 5
    valid = lambda a, b: (a + 1) * (b + 1) <= k
    pieces = []
    for b in range(min(split, k)):
        for a0 in range(0, k, 8):
            if any(valid(a, b) for a in range(a0, a0 + 8)):
                pieces.append(("col", a0, b))
    for a in range(k):
        for b0 in range(0, k, 8):
            if any(valid(a, b) and b >= split for b in range(b0, b0 + 8)):
                pieces.append(("row", a, b0))
    covered = []
    for kind, p, q in pieces:
        for r in range(8):
            a, b = (p + r, q) if kind == "col" else (p, q + r)
            if valid(a, b) and (kind == "col" or b >= split):
                covered.append((a, b))
    assert sorted(covered) == sorted((a, b) for a in range(k) for b in range(k) if valid(a, b))
    return pieces, split


def _pair_candidates(sv, si, n_keys, k):
    tok = sv[0].shape[1]
    pieces, split = _candidate_pieces(k)
    r8 = lax.broadcasted_iota(jnp.int32, (8, tok), 0)
    vals, order, eid = [], [], []
    for piece, (kind, p, q) in enumerate(pieces):
        if kind == "col":
            a_idx, b_idx = p + r8, jnp.full((8, tok), q, jnp.int32)
            v = sv[0][p:p + 8] + sv[1][q:q + 1]
            e = si[0][p:p + 8] * n_keys + si[1][q:q + 1]
            ok = (a_idx + 1) * (q + 1) <= k
        else:
            a_idx, b_idx = jnp.full((8, tok), p, jnp.int32), q + r8
            v = sv[0][p:p + 1] + sv[1][q:q + 8]
            e = si[0][p:p + 1] * n_keys + si[1][q:q + 8]
            ok = ((p + 1) * (b_idx + 1) <= k) & (b_idx >= split)
        vals.append(jnp.where(ok, v, -jnp.inf))
        unused = k * k + piece * 8 + r8
        order.append(jnp.where(ok, a_idx * k + b_idx, unused).astype(F32))
        eid.append(e)
    return (jnp.concatenate(vals, axis=0), jnp.concatenate(order, axis=0),
            jnp.concatenate(eid, axis=0))


def _route_kernel(q_ref, sk_ref, g_ref, it_ref, jt_ref, gt_ref, w_ref, *, heads, n_keys):
    tok = q_ref.shape[0]
    half = sk_ref.shape[-1]
    K = PEER_TOPK
    nt = (((1,), (1,)), ((), ()))
    key_id = lax.broadcasted_iota(jnp.int32, (n_keys, tok), 0).astype(F32)
    i_rows, j_rows, g_rows = [], [], []
    for h in range(heads):
        sv, si = [], []
        for p in range(2):
            c = (h * 2 + p) * half
            qh = q_ref[:, c:c + half].astype(BF16)
            st = lax.dot_general(sk_ref[h, p], qh, nt, preferred_element_type=F32)
            v, i = _topk_rows(st, key_id, K)
            sv.append(v)
            si.append(i)
        cand, flat, cid = _pair_candidates(sv, si, n_keys, K)
        cv, eid = _topk_rows(cand, flat, K, payload=cid)
        e = jnp.exp(cv - cv[0:1])
        gate = e / jnp.sum(e, axis=0, keepdims=True)
        ei = jnp.floor(eid * (1.0 / n_keys))
        i_rows.append(ei)
        j_rows.append(eid - ei * n_keys)
        g_rows.append(gate)
    it_ref[...] = jnp.concatenate(i_rows, axis=0).T
    jt_ref[...] = jnp.concatenate(j_rows, axis=0).T
    gt_ref[...] = jnp.concatenate(g_rows, axis=0).T

    picks = heads * K
    sub = lax.broadcasted_iota(jnp.int32, (n_keys, picks), 0).astype(F32)

    def per_token(n, _):
        irow = it_ref[pl.ds(n, 1), :]
        jrow = jt_ref[pl.ds(n, 1), :]
        grow = gt_ref[pl.ds(n, 1), :]
        a_t = jnp.where(sub == irow, 1.0, 0.0).astype(BF16)
        b_t = jnp.where(sub == jrow, grow, 0.0).astype(BF16)
        r0 = pl.multiple_of(n * n_keys, n_keys)
        w_ref[pl.ds(r0, n_keys), :] = lax.dot_general(a_t, b_t, nt, preferred_element_type=F32)
        return 0

    lax.fori_loop(0, tok, per_token, 0, unroll=ROUTE_UNROLL)
    for i in range(n_keys):
        g_ref[:, i * n_keys:(i + 1) * n_keys] = w_ref[pl.ds(i, tok, stride=n_keys), :].astype(g_ref.dtype)


def _route(qp, subkeys_bf, *, name):
    n = qp.shape[0]
    heads, _, n_keys, half = subkeys_bf.shape
    tok = LANES
    picks = heads * PEER_TOPK
    return pl.pallas_call(
        functools.partial(_route_kernel, heads=heads, n_keys=n_keys),
        grid=(n // tok,),
        in_specs=[pl.BlockSpec((tok, qp.shape[1]), lambda i: (i, 0)),
                  pl.BlockSpec(subkeys_bf.shape, lambda i: (0, 0, 0, 0))],
        out_specs=pl.BlockSpec((tok, n_keys * n_keys), lambda i: (i, 0)),
        out_shape=jax.ShapeDtypeStruct((n, n_keys * n_keys), BF16),
        scratch_shapes=[pltpu.VMEM((tok, picks), F32)] * 3
        + [pltpu.VMEM((tok * n_keys, n_keys), F32)],
        compiler_params=_params("parallel"),
        name=name,
    )(qp, subkeys_bf)


def _peer_act_kernel(x_ref, u_ref, g_ref, p_ref):
    s = lax.dot_general(x_ref[...], u_ref[...].astype(BF16), (((1,), (1,)), ((), ())),
                        preferred_element_type=F32)
    act = 0.5 * s * (1.0 + lax.erf(s * (1.0 / math.sqrt(2.0))))
    p_ref[...] = (act * g_ref[...].astype(F32)).astype(p_ref.dtype)


def _peer_act(xb, u, g, *, layer, tn, te, name):
    n, d = xb.shape
    n_exp = u.shape[1]
    return pl.pallas_call(
        _peer_act_kernel,
        grid=(n // tn, n_exp // te),
        in_specs=[pl.BlockSpec((tn, d), lambda i, e: (i, 0), pipeline_mode=pl.Buffered(1)),
                  pl.BlockSpec((None, te, d), lambda i, e: (layer, e, 0)),
                  pl.BlockSpec((tn, te), lambda i, e: (i, e))],
        out_specs=pl.BlockSpec((tn, te), lambda i, e: (i, e)),
        out_shape=jax.ShapeDtypeStruct((n, n_exp), BF16),
        compiler_params=_params("parallel", "parallel"),
        name=name,
    )(xb, u, g)


def _kv_heads_kernel(*refs, depth):
    ins, outs = refs[:-2], refs[-2:]
    for l in range(depth):
        for t, dst in enumerate(outs):
            dst[l] = ins[l * 2 + t][...]


def _kv_prompt_heads(projs, *, batches, t_len, heads, col_k, col_v, name):
    depth = len(projs)
    specs, args = [], []
    for proj in projs:
        for col in (col_k, col_v):
            specs.append(pl.BlockSpec((t_len, HEAD_DIM),
                                      lambda b, h, cb=col // HEAD_DIM: (b, cb + h)))
            args.append(proj)
    dst = pl.BlockSpec((depth, None, None, t_len, HEAD_DIM), lambda b, h: (0, b, h, 0, 0))
    shape = jax.ShapeDtypeStruct((depth, batches, heads, t_len, HEAD_DIM), F32)
    return pl.pallas_call(
        functools.partial(_kv_heads_kernel, depth=depth),
        grid=(batches, heads),
        in_specs=specs,
        out_specs=[dst, dst],
        out_shape=[shape, shape],
        compiler_params=_params("parallel", "parallel"),
        name=name,
    )(*args)


def _layer(x, xb, st, w, big, *, layer, dims, alpha):
    bp, tp, bs, ts = dims
    tag = str(layer)
    n, d = x.shape
    rows_p = bp * tp
    conv_ch = w["conv_w"].shape[-1]
    a_heads = big["cache_k"].shape[2]
    m_heads = w["b_i"].shape[0]
    attn_w = a_heads * HEAD_DIM
    mlstm_w = m_heads * HEAD_DIM
    off_att = 2 * conv_ch
    off_mls = off_att + 3 * attn_w
    off_gate = off_mls + 4 * mlstm_w

    tn_big = _pick_tile(n, 2080, SUBLANES_BF16)
    tn_acc = _pick_tile(n, 1664, SUBLANES_BF16)
    proj = _matmul(xb, big["w_in_t"], layer=layer, width=off_gate, tn=tn_big, transposed=True,
                   tm=_pick_tile(off_gate, 512, LANES), name=f"proj{tag}")
    gates = _matmul(xb, big["w_gate"], layer=layer, width=LANES, tn=tn_big, tm=LANES,
                    name=f"gates{tag}")
    gates_t = gates[:, :2 * m_heads].T

    zeros = lambda *s: jnp.zeros(s, F32)
    conv_args = (w["conv_w"], w["conv_b"], w["conv_ln_g"], w["conv_ln_b"])
    ya_p, conv_p = _conv_group(proj, zeros(bp, w["conv_w"].shape[0] - 1, conv_ch), *conv_args,
                               row0=0, batches=bp, rows=tp, n_valid=tp, out_rows=n,
                               name=f"conv_p{tag}")
    ya_s, conv_s = _conv_group(proj, st["conv"], *conv_args, row0=rows_p, batches=bs,
                               rows=SAMPLE_ROWS, n_valid=ts, name=f"conv_s{tag}")

    yb_p = _attn_prompt(proj, batches=bp, t_len=tp, heads=a_heads, col0=off_att, out_rows=n,
                        name=f"attn_p{tag}")
    yb_s = _attn_sample(proj, big["cache_k"], big["cache_v"], layer=layer, row0=rows_p,
                        batches=bs, heads=a_heads, col0=off_att, n_valid=ts, name=f"attn_s{tag}")

    ml_args = (w["b_i"], w["b_f"], w["norm_g"])
    yc_p, c_p, n_p, m_p = _mlstm_group(
        proj, gates_t, *ml_args, zeros(bp, m_heads, HEAD_DIM, HEAD_DIM),
        zeros(bp, m_heads, HEAD_DIM), zeros(bp, m_heads),
        row0=0, batches=bp, rows=tp, n_valid=tp, col0=off_mls, out_rows=n,
        name=f"mlstm_p{tag}")
    yc_s, c_s, n_s, m_s = _mlstm_group(
        proj, gates_t, *ml_args, st["c"], st["n"], st["m"],
        row0=rows_p, batches=bs, rows=SAMPLE_ROWS, n_valid=ts, col0=off_mls,
        name=f"mlstm_s{tag}")

    rows_s = bs * SAMPLE_ROWS
    tail = lambda a: jnp.pad(a, ((0, n - rows_p - rows_s), (0, 0)))
    y_parts = _fill_tail([ya_p, yb_p, yc_p], [tail(ya_s), tail(yb_s), tail(yc_s)], row0=rows_p,
                         name=f"ymix_tail{tag}")
    mix = _matmul_parts(y_parts, big["w_out"], layer=layer, tn=tn_big,
                        tm=_pick_tile(d, 512, LANES), name=f"outproj{tag}")
    tn_ln = _pick_tile(n, 256, SUBLANES_BF16)
    x1, x1b = _res_ln(x, mix, w["ln1_g"], w["ln1_b"], alpha=alpha, tn=tn_ln, name=f"ln1{tag}")

    q_dim = big["peer_wq"].shape[2]
    qp = _matmul(x1b, big["peer_wq"], layer=layer, width=q_dim, tn=tn_big,
                 tm=_pick_tile(q_dim, 512, LANES), name=f"peer_q{tag}")
    g = _route(qp, w["peer_subkeys"], name=f"route{tag}")
    n_exp = big["peer_u"].shape[1]
    p = _peer_act(x1b, big["peer_u"], g, layer=layer, tn=tn_big,
                  te=_pick_tile(n_exp, 512, LANES), name=f"peer_act{tag}")
    po = _matmul_ktiled(p, big["peer_v"], layer=layer, tn=tn_acc, tm=_pick_tile(d, 1024, LANES),
                        tk=_pick_tile(n_exp, 2048, LANES), name=f"peer_out{tag}")
    x2, x2b = _res_ln(x1, po, w["ln2_g"], w["ln2_b"], alpha=alpha, tn=tn_ln, name=f"ln2{tag}")

    kv = lambda rows, b, t, which: rows[:, off_att + which * attn_w:off_att + (which + 1) * attn_w
                                        ].reshape(b, t, a_heads, HEAD_DIM)
    sr = proj[rows_p:rows_p + rows_s]
    outs = dict(
        proj=proj, kv_cols=(off_att + attn_w, off_att + 2 * attn_w),
        k_s=kv(sr, bs, SAMPLE_ROWS, 1)[:, :ts], v_s=kv(sr, bs, SAMPLE_ROWS, 2)[:, :ts],
        conv_p=conv_p, conv_s=conv_s, c_p=c_p, c_s=c_s, n_p=n_p, n_s=n_s, m_p=m_p, m_s=m_s)
    return x2, x2b, outs


def kernel(x_prompt, x_sample, cache_attn_k, cache_attn_v, state_conv, state_mlstm_c, state_mlstm_n, state_mlstm_m, w_in, conv_w, conv_b, conv_ln_g, conv_ln_b, mlstm_b_i, mlstm_b_f, mlstm_norm_g, w_out, ln1_g, ln1_b, peer_wq, peer_subkeys, peer_u, peer_v, ln2_g, ln2_b):
    bp, tp, d = x_prompt.shape
    bs, ts, _ = x_sample.shape
    depth = w_in.shape[0]
    assert ts <= SAMPLE_ROWS and tp % SAMPLE_ROWS == 0
    rows_p = bp * tp
    rows_s = bs * SAMPLE_ROWS
    n = -(-(rows_p + rows_s) // LANES) * LANES
    xs = jnp.pad(x_sample, ((0, 0), (0, SAMPLE_ROWS - ts), (0, 0))).reshape(rows_s, d)
    x = jnp.concatenate([x_prompt.reshape(rows_p, d), xs,
                         jnp.zeros((n - rows_p - rows_s, d), x_prompt.dtype)], axis=0)
    xb = x.astype(BF16)
    alpha = (2 * depth) ** 0.25

    n_gate = 2 * mlstm_b_i.shape[1]
    w_gate = w_in[:, :, w_in.shape[2] - n_gate:].astype(BF16)
    big = dict(
        w_in_t=jnp.transpose(w_in, (0, 2, 1)),
        w_gate=jnp.pad(w_gate, ((0, 0), (0, 0), (0, LANES - n_gate))),
        w_out=w_out, peer_wq=peer_wq, peer_u=peer_u, peer_v=peer_v,
        cache_k=jnp.transpose(cache_attn_k, (0, 1, 3, 2, 4)),
        cache_v=jnp.transpose(cache_attn_v, (0, 1, 3, 2, 4)))
    subkeys_b = peer_subkeys.astype(BF16)

    per_layer = []
    for l in range(depth):
        st = dict(conv=state_conv[l], c=state_mlstm_c[l], n=state_mlstm_n[l], m=state_mlstm_m[l])
        w = dict(conv_w=conv_w[l], conv_b=conv_b[l], conv_ln_g=conv_ln_g[l],
                 conv_ln_b=conv_ln_b[l], b_i=mlstm_b_i[l], b_f=mlstm_b_f[l],
                 norm_g=mlstm_norm_g[l], ln1_g=ln1_g[l], ln1_b=ln1_b[l],
                 peer_subkeys=subkeys_b[l], ln2_g=ln2_g[l], ln2_b=ln2_b[l])
        x, xb, outs = _layer(x, xb, st, w, big, layer=l, dims=(bp, tp, bs, ts), alpha=alpha)
        per_layer.append(outs)

    a_heads = cache_attn_k.shape[3]
    col_k, col_v = per_layer[0]["kv_cols"]
    k_all, v_all = _kv_prompt_heads([o["proj"] for o in per_layer], batches=bp, t_len=tp,
                                    heads=a_heads, col_k=col_k, col_v=col_v, name="kv_prompt")
    win = min(max(wd for wd, _ in DILATED_PATTERNS), tp)
    k_prompt = jnp.transpose(k_all, (0, 1, 3, 2, 4))[:, :, tp - win:]
    v_prompt = jnp.transpose(v_all, (0, 1, 3, 2, 4))[:, :, tp - win:]

    stack = lambda key: jnp.stack([o[key] for o in per_layer])
    y_prompt = x[:rows_p].reshape(bp, tp, d)
    y_sample = x[rows_p:rows_p + rows_s].reshape(bs, SAMPLE_ROWS, d)[:, :ts]
    return (y_prompt, y_sample, k_prompt, v_prompt, stack("k_s"), stack("v_s"),
            stack("conv_p"), stack("conv_s"), stack("c_p"), stack("c_s"),
            stack("n_p"), stack("n_s"), stack("m_p"), stack("m_s"))
```

```python
import functools
import math

import jax
import jax.numpy as jnp
from jax import lax
from jax.experimental import pallas as pl
from jax.experimental.pallas import tpu as pltpu

F32 = jnp.float32
BF16 = jnp.bfloat16

LANES = 128
SUBLANES_BF16 = 16
VMEM_LIMIT_BYTES = 56 * 1024 * 1024

HEAD_DIM = 128
DILATED_PATTERNS = ((128, 1), (512, 4), (2048, 16))
MLSTM_CHUNK = 64
PEER_TOPK = 16
LN_EPS = 1e-5
SAMPLE_ROWS = 16
CONV_HALO = 32
CONV_ROWS = 64
MASKED = -1e30
ROUTE_UNROLL = 8
MLSTM_HEAD_GROUPS = (4, 3, 2, 1)
MLSTM_BLOCK_CHUNKS = 32


def _params(*sem):
    return pltpu.CompilerParams(dimension_semantics=sem,
                                vmem_limit_bytes=VMEM_LIMIT_BYTES)


def _pick_tile(n, target, align):
    best = None
    for t in range(align, min(n, target) + 1, align):
        if n % t == 0:
            best = t
    assert best is not None, (n, target, align)
    return best


def _mm_kernel(a_ref, w_ref, o_ref, *, transposed):
    dims = (((1,), (1,)), ((), ())) if transposed else (((1,), (0,)), ((), ()))
    o_ref[...] = lax.dot_general(a_ref[...], w_ref[...].astype(BF16), dims,
                                 preferred_element_type=F32)


def _matmul(a, w, *, layer, width, tn, tm, name, transposed=False):
    n, k = a.shape
    if transposed:
        w_spec = pl.BlockSpec((None, tm, k), lambda i, j: (layer, j, 0))
    else:
        w_spec = pl.BlockSpec((None, k, tm), lambda i, j: (layer, 0, j))
    return pl.pallas_call(
        functools.partial(_mm_kernel, transposed=transposed),
        grid=(n // tn, width // tm),
        in_specs=[pl.BlockSpec((tn, k), lambda i, j: (i, 0), pipeline_mode=pl.Buffered(1)),
                  w_spec],
        out_specs=pl.BlockSpec((tn, tm), lambda i, j: (i, j)),
        out_shape=jax.ShapeDtypeStruct((n, width), F32),
        compiler_params=_params("parallel", "parallel"),
        name=name,
    )(a, w)


def _mm_acc_kernel(a_ref, w_ref, o_ref):
    part = jnp.dot(a_ref[...], w_ref[...].astype(BF16), preferred_element_type=F32)

    @pl.when(pl.program_id(2) == 0)
    def _():
        o_ref[...] = part

    @pl.when(pl.program_id(2) > 0)
    def _():
        o_ref[...] += part


def _matmul_ktiled(a, w, *, layer, tn, tm, tk, name):
    n, k = a.shape
    m = w.shape[2]
    return pl.pallas_call(
        _mm_acc_kernel,
        grid=(n // tn, m // tm, k // tk),
        in_specs=[pl.BlockSpec((tn, tk), lambda i, j, q: (i, q)),
                  pl.BlockSpec((None, tk, tm), lambda i, j, q: (layer, q, j))],
        out_specs=pl.BlockSpec((tn, tm), lambda i, j, q: (i, j)),
        out_shape=jax.ShapeDtypeStruct((n, m), F32),
        compiler_params=_params("parallel", "parallel", "arbitrary"),
        name=name,
    )(a, w)


def _mm_parts_kernel(*refs, parts):
    a_refs, w_ref, o_ref = refs[:parts], refs[parts], refs[parts + 1]
    acc, lo = None, 0
    for a_ref in a_refs:
        hi = lo + a_ref.shape[1]
        part = jnp.dot(a_ref[...], w_ref[lo:hi, :].astype(BF16), preferred_element_type=F32)
        acc = part if acc is None else acc + part
        lo = hi
    o_ref[...] = acc


def _matmul_parts(a_parts, w, *, layer, tn, tm, name):
    n = a_parts[0].shape[0]
    k, m = w.shape[1], w.shape[2]
    assert sum(a.shape[1] for a in a_parts) == k
    a_specs = [pl.BlockSpec((tn, a.shape[1]), lambda i, j: (i, 0), pipeline_mode=pl.Buffered(1))
               for a in a_parts]
    return pl.pallas_call(
        functools.partial(_mm_parts_kernel, parts=len(a_parts)),
        grid=(n // tn, m // tm),
        in_specs=a_specs + [pl.BlockSpec((None, k, tm), lambda i, j: (layer, 0, j))],
        out_specs=pl.BlockSpec((tn, tm), lambda i, j: (i, j)),
        out_shape=jax.ShapeDtypeStruct((n, m), F32),
        compiler_params=_params("parallel", "parallel"),
        name=name,
    )(*a_parts, w)


def _fill_tail_kernel(*refs, count):
    for src, out in zip(refs[:count], refs[2 * count:]):
        out[...] = src[...]


def _fill_tail(dsts, srcs, *, row0, name):
    tail = srcs[0].shape[0]
    assert row0 % tail == 0 and all(d.shape[0] == row0 + tail for d in dsts)
    count = len(dsts)
    return pl.pallas_call(
        functools.partial(_fill_tail_kernel, count=count),
        grid=(1,),
        in_specs=[pl.BlockSpec(s.shape, lambda i: (0, 0)) for s in srcs]
        + [pl.BlockSpec(memory_space=pl.ANY)] * count,
        out_specs=[pl.BlockSpec((tail, d.shape[1]), lambda i: (row0 // tail, 0)) for d in dsts],
        out_shape=[jax.ShapeDtypeStruct(d.shape, d.dtype) for d in dsts],
        input_output_aliases={count + i: i for i in range(count)},
        compiler_params=_params("arbitrary"),
        name=name,
    )(*srcs, *dsts)


def _res_ln_kernel(x_ref, y_ref, g_ref, b_ref, o_ref, ob_ref, *, alpha):
    z = alpha * x_ref[...] + y_ref[...]
    mu = jnp.mean(z, axis=-1, keepdims=True)
    zc = z - mu
    var = jnp.mean(zc * zc, axis=-1, keepdims=True)
    out = zc * lax.rsqrt(var + LN_EPS) * g_ref[...] + b_ref[...]
    o_ref[...] = out
    ob_ref[...] = out.astype(BF16)


def _res_ln(x, y, g, b, *, alpha, tn, name):
    n, d = x.shape
    row = pl.BlockSpec((tn, d), lambda i: (i, 0))
    vec = pl.BlockSpec((1, d), lambda i: (0, 0))
    return pl.pallas_call(
        functools.partial(_res_ln_kernel, alpha=alpha),
        grid=(n // tn,),
        in_specs=[row, row, vec, vec],
        out_specs=[row, row],
        out_shape=[jax.ShapeDtypeStruct((n, d), F32),
                   jax.ShapeDtypeStruct((n, d), BF16)],
        compiler_params=_params("parallel"),
        name=name,
    )(x, y, g.reshape(1, d), b.reshape(1, d))


def _conv_kernel(a_ref, gate_ref, prev_ref, cw_ref, cb_ref, lg_ref, lb_ref,
                 y_ref, new_ref, ext_ref, *, tt, n_valid_last, width):
    t = pl.program_id(1)
    keep = width - 1

    @pl.when(t == 0)
    def _():
        ext_ref[0:CONV_HALO, :] = jnp.zeros((CONV_HALO, ext_ref.shape[1]), F32)
        ext_ref[CONV_HALO - keep:CONV_HALO, :] = prev_ref[0]

    ext_ref[CONV_HALO:CONV_HALO + tt, :] = a_ref[...] * jax.nn.sigmoid(gate_ref[...])
    rows = CONV_ROWS if tt % CONV_ROWS == 0 else tt
    pieces = []
    for r in range(0, tt, rows):
        cols = []
        for c in range(0, ext_ref.shape[1], LANES):
            acc = jnp.zeros((rows, LANES), F32)
            for w in range(width):
                lo = CONV_HALO - keep + w + r
                acc = acc + ext_ref[lo:lo + rows, c:c + LANES] * cw_ref[w:w + 1, c:c + LANES]
            cols.append(acc)
        pieces.append(jnp.concatenate(cols, axis=1))
    y = jnp.concatenate(pieces, axis=0) + cb_ref[...]
    mu = jnp.mean(y, axis=-1, keepdims=True)
    yc = y - mu
    var = jnp.mean(yc * yc, axis=-1, keepdims=True)
    y = yc * lax.rsqrt(var + LN_EPS) * lg_ref[...] + lb_ref[...]
    y_ref[...] = (y * jax.nn.sigmoid(y)).astype(y_ref.dtype)

    @pl.when(t == pl.num_programs(1) - 1)
    def _():
        hi = CONV_HALO + n_valid_last
        new_ref[0] = ext_ref[hi - keep:hi, :]

    ext_ref[0:CONV_HALO, :] = ext_ref[tt:tt + CONV_HALO, :]


def _conv_group(proj, prev, cw, cb, lg, lb, *, row0, batches, rows, n_valid, name, out_rows=None):
    width, ch = cw.shape
    tt = _pick_tile(rows, 256, 8)
    nt = rows // tt
    blk0 = row0 // tt
    assert row0 % tt == 0 and n_valid > rows - tt
    n_valid_last = n_valid - (nt - 1) * tt
    rowspec = lambda col: pl.BlockSpec((tt, ch), lambda b, t: (blk0 + b * nt + t, col))
    vec = pl.BlockSpec((1, ch), lambda b, t: (0, 0))
    state = pl.BlockSpec((1, width - 1, ch), lambda b, t: (b, 0, 0))
    return pl.pallas_call(
        functools.partial(_conv_kernel, tt=tt, n_valid_last=n_valid_last, width=width),
        grid=(batches, nt),
        in_specs=[rowspec(0), rowspec(1), state,
                  pl.BlockSpec((width, ch), lambda b, t: (0, 0)), vec, vec, vec],
        out_specs=[pl.BlockSpec((tt, ch), lambda b, t: (b * nt + t, 0)), state],
        out_shape=[jax.ShapeDtypeStruct((out_rows or batches * rows, ch), BF16),
                   jax.ShapeDtypeStruct((batches, width - 1, ch), F32)],
        scratch_shapes=[pltpu.VMEM((CONV_HALO + tt + CONV_HALO, ch), F32)],
        compiler_params=_params("parallel", "arbitrary"),
        name=name,
    )(proj, proj, prev, cw, cb.reshape(1, ch), lg.reshape(1, ch), lb.reshape(1, ch))


def _pattern_count(delta):
    cnt = jnp.zeros(delta.shape, F32)
    for window, dil in DILATED_PATTERNS:
        hit = (delta <= window) & ((delta & (dil - 1)) == 0)
        cnt = cnt + jnp.where(hit, 1.0, 0.0)
    return jnp.where(delta >= 0, cnt, 0.0)


def _attn_prompt_kernel(q_ref, k_ref, v_ref, o_ref, cnt_ref, *, qb, scale):
    t_len = q_ref.shape[0]

    @pl.when((pl.program_id(0) == 0) & (pl.program_id(1) == 0))
    def _():
        for i in range(t_len // qb):
            hi = (i + 1) * qb
            qpos = i * qb + lax.broadcasted_iota(jnp.int32, (qb, hi), 0)
            kpos = lax.broadcasted_iota(jnp.int32, (qb, hi), 1)
            cnt_ref[i * qb:hi, 0:hi] = _pattern_count(qpos - kpos)

    kb = k_ref[...].astype(BF16)
    vb = v_ref[...].astype(BF16)
    for i in range(t_len // qb):
        hi = (i + 1) * qb
        q = q_ref[i * qb:hi, :].astype(BF16)
        s = lax.dot_general(q, kb[0:hi], (((1,), (1,)), ((), ())),
                            preferred_element_type=F32) * scale
        cnt = cnt_ref[i * qb:hi, 0:hi]
        s = jnp.where(cnt > 0, s, MASKED)
        mx = jnp.max(s, axis=-1, keepdims=True)
        e = jnp.exp(s - mx) * cnt
        den = jnp.sum(e, axis=-1, keepdims=True)
        out = jnp.dot(e.astype(BF16), vb[0:hi], preferred_element_type=F32)
        o_ref[i * qb:hi, :] = (out / den).astype(o_ref.dtype)


def _attn_prompt(proj, *, batches, t_len, heads, col0, name, out_rows=None):
    c0 = col0 // HEAD_DIM
    qb = _pick_tile(t_len, 256, 8)
    spec = lambda which: pl.BlockSpec(
        (t_len, HEAD_DIM), lambda b, h: (b, c0 + which * heads + h))
    return pl.pallas_call(
        functools.partial(_attn_prompt_kernel, qb=qb, scale=HEAD_DIM ** -0.5),
        grid=(batches, heads),
        in_specs=[spec(0), spec(1), spec(2)],
        out_specs=pl.BlockSpec((t_len, HEAD_DIM), lambda b, h: (b, h)),
        out_shape=jax.ShapeDtypeStruct((out_rows or batches * t_len, heads * HEAD_DIM), BF16),
        scratch_shapes=[pltpu.VMEM((t_len, t_len), F32)],
        compiler_params=_params("arbitrary", "arbitrary"),
        name=name,
    )(proj, proj, proj)


def _attn_sample_kernel(q_ref, k_ref, v_ref, kc_ref, vc_ref, o_ref, *, n_valid, scale):
    rows = q_ref.shape[0]
    n_prev = kc_ref.shape[0]
    q = q_ref[...].astype(BF16)
    nt = (((1,), (1,)), ((), ()))
    s_c = lax.dot_general(q, kc_ref[...].astype(BF16), nt, preferred_element_type=F32) * scale
    s_n = lax.dot_general(q, k_ref[...].astype(BF16), nt, preferred_element_type=F32) * scale
    qpos_c = n_prev + lax.broadcasted_iota(jnp.int32, (rows, n_prev), 0)
    cnt_c = _pattern_count(qpos_c - lax.broadcasted_iota(jnp.int32, (rows, n_prev), 1))
    tq = lax.broadcasted_iota(jnp.int32, (rows, rows), 0)
    tk = lax.broadcasted_iota(jnp.int32, (rows, rows), 1)
    cnt_n = jnp.where(tk < n_valid, _pattern_count(tq - tk), 0.0)
    s_c = jnp.where(cnt_c > 0, s_c, MASKED)
    s_n = jnp.where(cnt_n > 0, s_n, MASKED)
    mx = jnp.maximum(jnp.max(s_c, axis=-1, keepdims=True),
                     jnp.max(s_n, axis=-1, keepdims=True))
    e_c = jnp.exp(s_c - mx) * cnt_c
    e_n = jnp.exp(s_n - mx) * cnt_n
    den = jnp.sum(e_c, axis=-1, keepdims=True) + jnp.sum(e_n, axis=-1, keepdims=True)
    out = (jnp.dot(e_c.astype(BF16), vc_ref[...].astype(BF16), preferred_element_type=F32)
           + jnp.dot(e_n.astype(BF16), v_ref[...].astype(BF16), preferred_element_type=F32))
    o_ref[...] = (out / den).astype(o_ref.dtype)


def _attn_sample(proj, cache_k, cache_v, *, layer, row0, batches, heads, col0, n_valid, name):
    c0 = col0 // HEAD_DIM
    r0 = row0 // SAMPLE_ROWS
    n_prev = cache_k.shape[3]
    new = lambda which: pl.BlockSpec(
        (SAMPLE_ROWS, HEAD_DIM), lambda b, h: (r0 + b, c0 + which * heads + h))
    cache = pl.BlockSpec((None, None, None, n_prev, HEAD_DIM), lambda b, h: (layer, b, h, 0, 0))
    return pl.pallas_call(
        functools.partial(_attn_sample_kernel, n_valid=n_valid, scale=HEAD_DIM ** -0.5),
        grid=(batches, heads),
        in_specs=[new(0), new(1), new(2), cache, cache],
        out_specs=pl.BlockSpec((SAMPLE_ROWS, HEAD_DIM), lambda b, h: (b, h)),
        out_shape=jax.ShapeDtypeStruct((batches * SAMPLE_ROWS, heads * HEAD_DIM), BF16),
        compiler_params=_params("parallel", "parallel"),
        name=name,
    )(proj, proj, proj, cache_k, cache_v)


def _mlstm_kernel(*refs, chunk, n_chunks, n_valid, hb, unroll):
    tok_refs = refs[:4 * hb]
    (gi_ref, gf_ref, bi_ref, bf_ref, ng_ref, c0_ref, n0_ref, m0_ref,
     y_ref, c_ref, n_ref, m_ref) = refs[4 * hb:]
    L = chunk
    row = lax.broadcasted_iota(jnp.int32, (L, L), 0)
    col = lax.broadcasted_iota(jnp.int32, (L, L), 1)
    causal = col <= row
    eye = jnp.where(row == col, 1.0, 0.0)
    upper = jnp.where(row <= col, 1.0, 0.0)
    lower = jnp.where(causal, 1.0, 0.0)
    lane = lax.broadcasted_iota(jnp.int32, (1, L), 1)
    tn = (((0,), (0,)), ((), ()))
    nt = (((1,), (1,)), ((), ()))

    def to_col(r):
        return jnp.sum(eye * r, axis=-1, keepdims=True)

    def one_head(j, c, r0, cs, ns, ms):
        q_ref, k_ref, v_ref, og_ref = tok_refs[4 * j:4 * j + 4]
        q = q_ref[pl.ds(r0, L), :]
        k = k_ref[pl.ds(r0, L), :] * (HEAD_DIM ** -0.5)
        v = v_ref[pl.ds(r0, L), :]
        i_row = gi_ref[j, 0, pl.ds(c, 1), :] + bi_ref[j]
        f_pre = gf_ref[j, 0, pl.ds(c, 1), :] + bf_ref[j]
        f_row = jnp.minimum(f_pre, 0.0) - jnp.log(1.0 + jnp.exp(-jnp.abs(f_pre)))
        if n_valid < L:
            i_row = jnp.where(lane < n_valid, i_row, MASKED)
            f_row = jnp.where(lane < n_valid, f_row, 0.0)
        qb = q.astype(BF16)
        kb = k.astype(BF16)
        vb = v.astype(BF16)
        qk_raw = lax.dot_general(qb, kb, nt, preferred_element_type=F32)
        q_c = jnp.dot(qb, cs.astype(BF16), preferred_element_type=F32)
        yield
        i_col = to_col(i_row)
        f_col = to_col(f_row)
        b_col = jnp.sum(lower * f_row, axis=-1, keepdims=True)
        yield
        b_row = jnp.sum(upper * f_col, axis=0, keepdims=True)
        a_intra = jnp.where(causal, b_col - b_row + i_row, MASKED)
        a_inter = b_col + ms
        m_t = jnp.maximum(a_inter, jnp.max(a_intra, axis=-1, keepdims=True))
        yield
        w_intra = jnp.exp(a_intra - m_t)
        w_inter = jnp.exp(a_inter - m_t)
        qk = qk_raw * w_intra
        num = w_inter * q_c + jnp.dot(qk.astype(BF16), vb, preferred_element_type=F32)
        qn = (w_inter * jnp.sum(q * ns, axis=-1, keepdims=True)
              + jnp.sum(qk, axis=-1, keepdims=True))
        yield
        h = num / jnp.maximum(jnp.abs(qn), jnp.exp(-m_t))
        m_new = m_t[L - 1:L, :]
        b_last = b_col[L - 1:L, :]
        decay = jnp.exp(b_last + ms - m_new)
        g_col = jnp.exp(b_last - b_col + i_col - m_new)
        kg = k * g_col
        c_new = decay * cs + lax.dot_general(kg.astype(BF16), vb, tn,
                                             preferred_element_type=F32)
        n_new = decay * ns + jnp.sum(kg, axis=0, keepdims=True)
        yield
        hg = jax.nn.sigmoid(og_ref[pl.ds(r0, L), :]) * h
        mu = jnp.mean(hg, axis=-1, keepdims=True)
        hc = hg - mu
        var = jnp.mean(hc * hc, axis=-1, keepdims=True)
        y_ref[pl.ds(r0, L), j * HEAD_DIM:(j + 1) * HEAD_DIM] = (
            hc * lax.rsqrt(var + LN_EPS) * ng_ref[j]).astype(y_ref.dtype)
        return c_new, n_new, m_new

    def body(c, carry):
        r0 = pl.multiple_of(c * L, L)
        gens = [one_head(j, c, r0, *carry[3 * j:3 * j + 3]) for j in range(hb)]
        results = [None] * hb
        active = list(range(hb))
        while active:
            for j in list(active):
                try:
                    next(gens[j])
                except StopIteration as done:
                    results[j] = done.value
                    active.remove(j)
        return tuple(v for r in results for v in r)

    @pl.when(pl.program_id(2) == 0)
    def _():
        c_ref[...] = c0_ref[...]
        n_ref[...] = n0_ref[...]
        m_ref[...] = m0_ref[...]

    init = []
    for j in range(hb):
        init.extend((c_ref[0, j], n_ref[0, j], m_ref[0, j]))
    final = lax.fori_loop(0, n_chunks, body, tuple(init), unroll=unroll)
    for j in range(hb):
        c_ref[0, j] = final[3 * j]
        n_ref[0, j] = final[3 * j + 1]
        m_ref[0, j] = final[3 * j + 2]


def _mlstm_group(proj, gates_t, b_i, b_f, norm_g, c0, n0, m0,
                 *, row0, batches, rows, n_valid, col0, name, out_rows=None):
    heads = b_i.shape[0]
    hb = max(g for g in MLSTM_HEAD_GROUPS if heads % g == 0)
    chunk = MLSTM_CHUNK if n_valid % MLSTM_CHUNK == 0 else rows
    assert n_valid == rows or chunk == rows
    n_chunks = rows // chunk
    blk_chunks = _pick_tile(n_chunks, MLSTM_BLOCK_CHUNKS, 8) if n_chunks % 8 == 0 else n_chunks
    tb = blk_chunks * chunk
    nt = rows // tb
    c0b = col0 // HEAD_DIM
    rb = row0 // tb
    g4 = gates_t[:, row0:row0 + batches * rows].reshape(2 * heads, batches, n_chunks, chunk)

    def tok(which, j):
        return pl.BlockSpec((tb, HEAD_DIM),
                            lambda b, h, t: (rb + b * nt + t, c0b + which * heads + h * hb + j))

    gate = lambda off: pl.BlockSpec(
        (hb, 1, blk_chunks, chunk), lambda b, h, t: (off // hb + h, b, t, 0))
    scal = pl.BlockSpec((hb, 1, 1), lambda b, h, t: (h, 0, 0))
    cspec = pl.BlockSpec((1, hb, HEAD_DIM, HEAD_DIM), lambda b, h, t: (b, h, 0, 0))
    nspec = pl.BlockSpec((1, hb, 1, HEAD_DIM), lambda b, h, t: (b, h, 0, 0))
    mspec = pl.BlockSpec((1, hb, 1, 1), lambda b, h, t: (b, h, 0, 0))
    tok_specs = [tok(which, j) for j in range(hb) for which in range(4)]
    y, c, n, m = pl.pallas_call(
        functools.partial(_mlstm_kernel, chunk=chunk, n_chunks=blk_chunks,
                          n_valid=min(n_valid, chunk), hb=hb,
                          unroll=2 if blk_chunks % 2 == 0 else 1),
        grid=(batches, heads // hb, nt),
        in_specs=tok_specs + [gate(0), gate(heads), scal, scal,
                              pl.BlockSpec((hb, 1, HEAD_DIM), lambda b, h, t: (h, 0, 0)),
                              cspec, nspec, mspec],
        out_specs=[pl.BlockSpec((tb, hb * HEAD_DIM), lambda b, h, t: (b * nt + t, h)),
                   cspec, nspec, mspec],
        out_shape=[jax.ShapeDtypeStruct((out_rows or batches * rows, heads * HEAD_DIM), BF16),
                   jax.ShapeDtypeStruct(c0.shape, F32),
                   jax.ShapeDtypeStruct(n0.shape[:2] + (1, HEAD_DIM), F32),
                   jax.ShapeDtypeStruct(m0.shape + (1, 1), F32)],
        compiler_params=_params("parallel", "parallel", "arbitrary"),
        name=name,
    )(*([proj] * (4 * hb)), g4, g4,
      b_i.reshape(heads, 1, 1), b_f.reshape(heads, 1, 1), norm_g.reshape(heads, 1, HEAD_DIM),
      c0, n0.reshape(n0.shape[:2] + (1, HEAD_DIM)), m0.reshape(m0.shape + (1, 1)))
    return y, c, n.reshape(n0.shape), m.reshape(m0.shape)


def _topk_rows(vals, order, k, payload=None):
    out_v, out_p = [], []
    for _ in range(k):
        m = jnp.max(vals, axis=0, keepdims=True)
        first = jnp.min(jnp.where(vals == m, order, float(2 ** 20)), axis=0, keepdims=True)
        hit = order == first
        out_v.append(m)
        if payload is None:
            out_p.append(first)
        else:
            out_p.append(jnp.sum(jnp.where(hit, payload, 0.0), axis=0, keepdims=True))
        vals = jnp.where(hit, -jnp.inf, vals)
    return jnp.concatenate(out_v, axis=0), jnp.concatenate(out_p, axis=0)


def _candidate_pieces(k):
    split = 5
    valid = lambda a, b: (a + 1) * (b + 1) <= k
    pieces = []
    for b in range(min(split, k)):
        for a0 in range(0, k, 8):
            if any(valid(a, b) for a in range(a0, a0 + 8)):
                pieces.append(("col", a0, b))
    for a in range(k):
        for b0 in range(0, k, 8):
            if any(valid(a, b) and b >= split for b in range(b0, b0 + 8)):
                pieces.append(("row", a, b0))
    covered = []
    for kind, p, q in pieces:
        for r in range(8):
            a, b = (p + r, q) if kind == "col" else (p, q + r)
            if valid(a, b) and (kind == "col" or b >= split):
                covered.append((a, b))
    assert sorted(covered) == sorted((a, b) for a in range(k) for b in range(k) if valid(a, b))
    return pieces, split


def _pair_candidates(sv, si, n_keys, k):
    tok = sv[0].shape[1]
    pieces, split = _candidate_pieces(k)
    r8 = lax.broadcasted_iota(jnp.int32, (8, tok), 0)
    vals, order, eid = [], [], []
    for piece, (kind, p, q) in enumerate(pieces):
        if kind == "col":
            a_idx, b_idx = p + r8, jnp.full((8, tok), q, jnp.int32)
            v = sv[0][p:p + 8] + sv[1][q:q + 1]
            e = si[0][p:p + 8] * n_keys + si[1][q:q + 1]
            ok = (a_idx + 1) * (q + 1) <= k
        else:
            a_idx, b_idx = jnp.full((8, tok), p, jnp.int32), q + r8
            v = sv[0][p:p + 1] + sv[1][q:q + 8]
            e = si[0][p:p + 1] * n_keys + si[1][q:q + 8]
            ok = ((p + 1) * (b_idx + 1) <= k) & (b_idx >= split)
        vals.append(jnp.where(ok, v, -jnp.inf))
        unused = k * k + piece * 8 + r8
        order.append(jnp.where(ok, a_idx * k + b_idx, unused).astype(F32))
        eid.append(e)
    return (jnp.concatenate(vals, axis=0), jnp.concatenate(order, axis=0),
            jnp.concatenate(eid, axis=0))


def _route_kernel(q_ref, sk_ref, g_ref, it_ref, jt_ref, gt_ref, w_ref, *, heads, n_keys):
    tok = q_ref.shape[0]
    half = sk_ref.shape[-1]
    K = PEER_TOPK
    nt = (((1,), (1,)), ((), ()))
    key_id = lax.broadcasted_iota(jnp.int32, (n_keys, tok), 0).astype(F32)
    i_rows, j_rows, g_rows = [], [], []
    for h in range(heads):
        sv, si = [], []
        for p in range(2):
            c = (h * 2 + p) * half
            qh = q_ref[:, c:c + half].astype(BF16)
            st = lax.dot_general(sk_ref[h, p], qh, nt, preferred_element_type=F32)
            v, i = _topk_rows(st, key_id, K)
            sv.append(v)
            si.append(i)
        cand, flat, cid = _pair_candidates(sv, si, n_keys, K)
        cv, eid = _topk_rows(cand, flat, K, payload=cid)
        e = jnp.exp(cv - cv[0:1])
        gate = e / jnp.sum(e, axis=0, keepdims=True)
        ei = jnp.floor(eid * (1.0 / n_keys))
        i_rows.append(ei)
        j_rows.append(eid - ei * n_keys)
        g_rows.append(gate)
    it_ref[...] = jnp.concatenate(i_rows, axis=0).T
    jt_ref[...] = jnp.concatenate(j_rows, axis=0).T
    gt_ref[...] = jnp.concatenate(g_rows, axis=0).T

    picks = heads * K
    sub = lax.broadcasted_iota(jnp.int32, (n_keys, picks), 0).astype(F32)

    def per_token(n, _):
        irow = it_ref[pl.ds(n, 1), :]
        jrow = jt_ref[pl.ds(n, 1), :]
        grow = gt_ref[pl.ds(n, 1), :]
        a_t = jnp.where(sub == irow, 1.0, 0.0).astype(BF16)
        b_t = jnp.where(sub == jrow, grow, 0.0).astype(BF16)
        r0 = pl.multiple_of(n * n_keys, n_keys)
        w_ref[pl.ds(r0, n_keys), :] = lax.dot_general(a_t, b_t, nt, preferred_element_type=F32)
        return 0

    lax.fori_loop(0, tok, per_token, 0, unroll=ROUTE_UNROLL)
    for i in range(n_keys):
        g_ref[:, i * n_keys:(i + 1) * n_keys] = w_ref[pl.ds(i, tok, stride=n_keys), :].astype(g_ref.dtype)


def _route(qp, subkeys_bf, *, name):
    n = qp.shape[0]
    heads, _, n_keys, half = subkeys_bf.shape
    tok = LANES
    picks = heads * PEER_TOPK
    return pl.pallas_call(
        functools.partial(_route_kernel, heads=heads, n_keys=n_keys),
        grid=(n // tok,),
        in_specs=[pl.BlockSpec((tok, qp.shape[1]), lambda i: (i, 0)),
                  pl.BlockSpec(subkeys_bf.shape, lambda i: (0, 0, 0, 0))],
        out_specs=pl.BlockSpec((tok, n_keys * n_keys), lambda i: (i, 0)),
        out_shape=jax.ShapeDtypeStruct((n, n_keys * n_keys), BF16),
        scratch_shapes=[pltpu.VMEM((tok, picks), F32)] * 3
        + [pltpu.VMEM((tok * n_keys, n_keys), F32)],
        compiler_params=_params("parallel"),
        name=name,
    )(qp, subkeys_bf)


def _peer_act_kernel(x_ref, u_ref, g_ref, p_ref):
    s = lax.dot_general(x_ref[...], u_ref[...].astype(BF16), (((1,), (1,)), ((), ())),
                        preferred_element_type=F32)
    act = 0.5 * s * (1.0 + lax.erf(s * (1.0 / math.sqrt(2.0))))
    p_ref[...] = (act * g_ref[...].astype(F32)).astype(p_ref.dtype)


def _peer_act(xb, u, g, *, layer, tn, te, name):
    n, d = xb.shape
    n_exp = u.shape[1]
    return pl.pallas_call(
        _peer_act_kernel,
        grid=(n // tn, n_exp // te),
        in_specs=[pl.BlockSpec((tn, d), lambda i, e: (i, 0), pipeline_mode=pl.Buffered(1)),
                  pl.BlockSpec((None, te, d), lambda i, e: (layer, e, 0)),
                  pl.BlockSpec((tn, te), lambda i, e: (i, e))],
        out_specs=pl.BlockSpec((tn, te), lambda i, e: (i, e)),
        out_shape=jax.ShapeDtypeStruct((n, n_exp), BF16),
        compiler_params=_params("parallel", "parallel"),
        name=name,
    )(xb, u, g)


def _kv_heads_kernel(*refs, depth):
    ins, outs = refs[:-2], refs[-2:]
    for l in range(depth):
        for t, dst in enumerate(outs):
            dst[l] = ins[l * 2 + t][...]


def _kv_prompt_heads(projs, *, batches, t_len, heads, col_k, col_v, name):
    depth = len(projs)
    specs, args = [], []
    for proj in projs:
        for col in (col_k, col_v):
            specs.append(pl.BlockSpec((t_len, HEAD_DIM),
                                      lambda b, h, cb=col // HEAD_DIM: (b, cb + h)))
            args.append(proj)
    dst = pl.BlockSpec((depth, None, None, t_len, HEAD_DIM), lambda b, h: (0, b, h, 0, 0))
    shape = jax.ShapeDtypeStruct((depth, batches, heads, t_len, HEAD_DIM), F32)
    return pl.pallas_call(
        functools.partial(_kv_heads_kernel, depth=depth),
        grid=(batches, heads),
        in_specs=specs,
        out_specs=[dst, dst],
        out_shape=[shape, shape],
        compiler_params=_params("parallel", "parallel"),
        name=name,
    )(*args)


def _layer(x, xb, st, w, big, *, layer, dims, alpha):
    bp, tp, bs, ts = dims
    tag = str(layer)
    n, d = x.shape
    rows_p = bp * tp
    conv_ch = w["conv_w"].shape[-1]
    a_heads = big["cache_k"].shape[2]
    m_heads = w["b_i"].shape[0]
    attn_w = a_heads * HEAD_DIM
    mlstm_w = m_heads * HEAD_DIM
    off_att = 2 * conv_ch
    off_mls = off_att + 3 * attn_w
    off_gate = off_mls + 4 * mlstm_w

    tn_big = _pick_tile(n, 2080, SUBLANES_BF16)
    tn_acc = _pick_tile(n, 1664, SUBLANES_BF16)
    proj = _matmul(xb, big["w_in_t"], layer=layer, width=off_gate, tn=tn_big, transposed=True,
                   tm=_pick_tile(off_gate, 512, LANES), name=f"proj{tag}")
    gates = _matmul(xb, big["w_gate"], layer=layer, width=LANES, tn=tn_big, tm=LANES,
                    name=f"gates{tag}")
    gates_t = gates[:, :2 * m_heads].T

    zeros = lambda *s: jnp.zeros(s, F32)
    conv_args = (w["conv_w"], w["conv_b"], w["conv_ln_g"], w["conv_ln_b"])
    ya_p, conv_p = _conv_group(proj, zeros(bp, w["conv_w"].shape[0] - 1, conv_ch), *conv_args,
                               row0=0, batches=bp, rows=tp, n_valid=tp, out_rows=n,
                               name=f"conv_p{tag}")
    ya_s, conv_s = _conv_group(proj, st["conv"], *conv_args, row0=rows_p, batches=bs,
                               rows=SAMPLE_ROWS, n_valid=ts, name=f"conv_s{tag}")

    yb_p = _attn_prompt(proj, batches=bp, t_len=tp, heads=a_heads, col0=off_att, out_rows=n,
                        name=f"attn_p{tag}")
    yb_s = _attn_sample(proj, big["cache_k"], big["cache_v"], layer=layer, row0=rows_p,
                        batches=bs, heads=a_heads, col0=off_att, n_valid=ts, name=f"attn_s{tag}")

    ml_args = (w["b_i"], w["b_f"], w["norm_g"])
    yc_p, c_p, n_p, m_p = _mlstm_group(
        proj, gates_t, *ml_args, zeros(bp, m_heads, HEAD_DIM, HEAD_DIM),
        zeros(bp, m_heads, HEAD_DIM), zeros(bp, m_heads),
        row0=0, batches=bp, rows=tp, n_valid=tp, col0=off_mls, out_rows=n,
        name=f"mlstm_p{tag}")
    yc_s, c_s, n_s, m_s = _mlstm_group(
        proj, gates_t, *ml_args, st["c"], st["n"], st["m"],
        row0=rows_p, batches=bs, rows=SAMPLE_ROWS, n_valid=ts, col0=off_mls,
        name=f"mlstm_s{tag}")

    rows_s = bs * SAMPLE_ROWS
    tail = lambda a: jnp.pad(a, ((0, n - rows_p - rows_s), (0, 0)))
    y_parts = _fill_tail([ya_p, yb_p, yc_p], [tail(ya_s), tail(yb_s), tail(yc_s)], row0=rows_p,
                         name=f"ymix_tail{tag}")
    mix = _matmul_parts(y_parts, big["w_out"], layer=layer, tn=tn_big,
                        tm=_pick_tile(d, 512, LANES), name=f"outproj{tag}")
    tn_ln = _pick_tile(n, 256, SUBLANES_BF16)
    x1, x1b = _res_ln(x, mix, w["ln1_g"], w["ln1_b"], alpha=alpha, tn=tn_ln, name=f"ln1{tag}")

    q_dim = big["peer_wq"].shape[2]
    qp = _matmul(x1b, big["peer_wq"], layer=layer, width=q_dim, tn=tn_big,
                 tm=_pick_tile(q_dim, 512, LANES), name=f"peer_q{tag}")
    g = _route(qp, w["peer_subkeys"], name=f"route{tag}")
    n_exp = big["peer_u"].shape[1]
    p = _peer_act(x1b, big["peer_u"], g, layer=layer, tn=tn_big,
                  te=_pick_tile(n_exp, 512, LANES), name=f"peer_act{tag}")
    po = _matmul_ktiled(p, big["peer_v"], layer=layer, tn=tn_acc, tm=_pick_tile(d, 1024, LANES),
                        tk=_pick_tile(n_exp, 2048, LANES), name=f"peer_out{tag}")
    x2, x2b = _res_ln(x1, po, w["ln2_g"], w["ln2_b"], alpha=alpha, tn=tn_ln, name=f"ln2{tag}")

    kv = lambda rows, b, t, which: rows[:, off_att + which * attn_w:off_att + (which + 1) * attn_w
                                        ].reshape(b, t, a_heads, HEAD_DIM)
    sr = proj[rows_p:rows_p + rows_s]
    outs = dict(
        proj=proj, kv_cols=(off_att + attn_w, off_att + 2 * attn_w),
        k_s=kv(sr, bs, SAMPLE_ROWS, 1)[:, :ts], v_s=kv(sr, bs, SAMPLE_ROWS, 2)[:, :ts],
        conv_p=conv_p, conv_s=conv_s, c_p=c_p, c_s=c_s, n_p=n_p, n_s=n_s, m_p=m_p, m_s=m_s)
    return x2, x2b, outs


def kernel(x_prompt, x_sample, cache_attn_k, cache_attn_v, state_conv, state_mlstm_c, state_mlstm_n, state_mlstm_m, w_in, conv_w, conv_b, conv_ln_g, conv_ln_b, mlstm_b_i, mlstm_b_f, mlstm_norm_g, w_out, ln1_g, ln1_b, peer_wq, peer_subkeys, peer_u, peer_v, ln2_g, ln2_b):
    bp, tp, d = x_prompt.shape
    bs, ts, _ = x_sample.shape
    depth = w_in.shape[0]
    assert ts <= SAMPLE_ROWS and tp % SAMPLE_ROWS == 0
    rows_p = bp * tp
    rows_s = bs * SAMPLE_ROWS
    n = -(-(rows_p + rows_s) // LANES) * LANES
    xs = jnp.pad(x_sample, ((0, 0), (0, SAMPLE_ROWS - ts), (0, 0))).reshape(rows_s, d)
    x = jnp.concatenate([x_prompt.reshape(rows_p, d), xs,
                         jnp.zeros((n - rows_p - rows_s, d), x_prompt.dtype)], axis=0)
    xb = x.astype(BF16)
    alpha = (2 * depth) ** 0.25

    n_gate = 2 * mlstm_b_i.shape[1]
    w_gate = w_in[:, :, w_in.shape[2] - n_gate:].astype(BF16)
    big = dict(
        w_in_t=jnp.transpose(w_in, (0, 2, 1)),
        w_gate=jnp.pad(w_gate, ((0, 0), (0, 0), (0, LANES - n_gate))),
        w_out=w_out, peer_wq=peer_wq, peer_u=peer_u, peer_v=peer_v,
        cache_k=jnp.transpose(cache_attn_k, (0, 1, 3, 2, 4)),
        cache_v=jnp.transpose(cache_attn_v, (0, 1, 3, 2, 4)))
    subkeys_b = peer_subkeys.astype(BF16)

    per_layer = []
    for l in range(depth):
        st = dict(conv=state_conv[l], c=state_mlstm_c[l], n=state_mlstm_n[l], m=state_mlstm_m[l])
        w = dict(conv_w=conv_w[l], conv_b=conv_b[l], conv_ln_g=conv_ln_g[l],
                 conv_ln_b=conv_ln_b[l], b_i=mlstm_b_i[l], b_f=mlstm_b_f[l],
                 norm_g=mlstm_norm_g[l], ln1_g=ln1_g[l], ln1_b=ln1_b[l],
                 peer_subkeys=subkeys_b[l], ln2_g=ln2_g[l], ln2_b=ln2_b[l])
        x, xb, outs = _layer(x, xb, st, w, big, layer=l, dims=(bp, tp, bs, ts), alpha=alpha)
        per_layer.append(outs)

    a_heads = cache_attn_k.shape[3]
    col_k, col_v = per_layer[0]["kv_cols"]
    k_all, v_all = _kv_prompt_heads([o["proj"] for o in per_layer], batches=bp, t_len=tp,
                                    heads=a_heads, col_k=col_k, col_v=col_v, name="kv_prompt")
    win = min(max(wd for wd, _ in DILATED_PATTERNS), tp)
    k_prompt = jnp.transpose(k_all, (0, 1, 3, 2, 4))[:, :, tp - win:]
    v_prompt = jnp.transpose(v_all, (0, 1, 3, 2, 4))[:, :, tp - win:]

    stack = lambda key: jnp.stack([o[key] for o in per_layer])
    y_prompt = x[:rows_p].reshape(bp, tp, d)
    y_sample = x[rows_p:rows_p + rows_s].reshape(bs, SAMPLE_ROWS, d)[:, :ts]
    return (y_prompt, y_sample, k_prompt, v_prompt, stack("k_s"), stack("v_s"),
            stack("conv_p"), stack("conv_s"), stack("c_p"), stack("c_s"),
            stack("n_p"), stack("n_s"), stack("m_p"), stack("m_s"))
```
